```python
import jax, jax.numpy as jnp
from jax import lax
import numpy as np

D_MODEL = 1024
BATCH = 2
SEQ = 16384
DEPTH = 4
DEC_BATCH = 32
DEC_SEQ = 2048
PAST_LEN = 128

EXPAND = 2
D_MIX = EXPAND * D_MODEL
D_ATTN = D_MIX // 2
D_POOL = D_MIX - D_ATTN
HEAD_DIM = 64
N_HEADS = D_ATTN // HEAD_DIM
DILATED_PATTERNS = ((128, 1), (512, 4), (2048, 16))
POOL_WINDOWS = (2, 4, 8, 16)
N_POOL_GROUPS = len(POOL_WINDOWS)
POOL_GROUP_DIM = D_POOL // N_POOL_GROUPS
ROPE_THETA = 10000.0
RMS_EPS = 1e-6
D_IN = 4 * D_ATTN + 2 * D_POOL

kernel_name = "hybrid_dilated_attn_pool_encoder"


def _rmsnorm(x, g):
    xf = x.astype(jnp.float32)
    y = xf * lax.rsqrt(jnp.mean(xf * xf, axis=-1, keepdims=True) + RMS_EPS)
    return (y * g.astype(jnp.float32)).astype(x.dtype)


def _rope(x):
    s, hd = x.shape[1], x.shape[3]
    inv_freq = ROPE_THETA ** (-jnp.arange(0, hd, 2, dtype=jnp.float32) / hd)
    ang = jnp.arange(s, dtype=jnp.float32)[:, None] * inv_freq[None, :]
    cos = jnp.cos(ang)[None, :, None, :]
    sin = jnp.sin(ang)[None, :, None, :]
    xf = x.astype(jnp.float32)
    x1, x2 = xf[..., : hd // 2], xf[..., hd // 2:]
    return jnp.concatenate([x1 * cos - x2 * sin, x1 * sin + x2 * cos], axis=-1).astype(x.dtype)


def _dilated_window_attention(q, k, v, dilation, half):
    b, s, h, hd = q.shape
    t = s // dilation
    blk = half
    nb = -(-t // blk)
    tp = nb * blk
    pad = tp - t

    def to_sub(a):
        return a.reshape(b, t, dilation, h, hd).transpose(0, 2, 3, 1, 4)

    qs = jnp.pad(to_sub(q), ((0, 0), (0, 0), (0, 0), (0, pad), (0, 0)))
    kpad = ((0, 0), (0, 0), (0, 0), (blk, pad + blk), (0, 0))
    ks = jnp.pad(to_sub(k), kpad)
    vs = jnp.pad(to_sub(v), kpad)
    qb = qs.reshape(b, dilation, h, nb, blk, hd)
    kb = ks.reshape(b, dilation, h, nb + 2, blk, hd)
    vb = vs.reshape(b, dilation, h, nb + 2, blk, hd)
    kn = jnp.concatenate([kb[:, :, :, :-2], kb[:, :, :, 1:-1], kb[:, :, :, 2:]], axis=4)
    vn = jnp.concatenate([vb[:, :, :, :-2], vb[:, :, :, 1:-1], vb[:, :, :, 2:]], axis=4)

    scores = jnp.einsum('brhnqd,brhnkd->brhnqk', qb, kn,
                        preferred_element_type=jnp.float32) * (hd ** -0.5)
    qi = jnp.arange(blk)[:, None]
    kj = jnp.arange(3 * blk)[None, :]
    rel = kj - blk - qi
    key_pos = jnp.arange(nb)[:, None, None] * blk + kj[None] - blk
    valid = (jnp.abs(rel) <= half)[None] & (key_pos >= 0) & (key_pos < t)
    scores = jnp.where(valid, scores, -jnp.inf)
    lse = jax.nn.logsumexp(scores, axis=-1)
    p = jnp.exp(scores - lse[..., None])
    out = jnp.einsum('brhnqk,brhnkd->brhnqd', p.astype(v.dtype), vn,
                     preferred_element_type=jnp.float32)
    out = out.reshape(b, dilation, h, tp, hd)[:, :, :, :t].transpose(0, 3, 1, 2, 4).reshape(b, s, h, hd)
    lse = lse.reshape(b, dilation, h, tp)[..., :t].transpose(0, 3, 1, 2).reshape(b, s, h)
    return out, lse


def _dilated_attention_mixture(q, k, v):
    outs, lses = [], []
    for window, dilation in DILATED_PATTERNS:
        o, l = _dilated_window_attention(q, k, v, dilation, window // (2 * dilation))
        outs.append(o)
        lses.append(l)
    w = jax.nn.softmax(jnp.stack(lses, axis=0), axis=0)
    return jnp.einsum('pbsh,pbshd->bshd', w, jnp.stack(outs, axis=0))


def _centred_mean_minus_identity(u, window):
    s = u.shape[1]
    uf = u.astype(jnp.float32)
    csum = jnp.pad(jnp.cumsum(uf, axis=1), ((0, 0), (1, 0), (0, 0)))
    pos = jnp.arange(s)
    lo = jnp.clip(pos - window // 2, 0, s)
    hi = jnp.clip(pos + window // 2, 0, s)
    total = jnp.take(csum, hi, axis=1) - jnp.take(csum, lo, axis=1)
    count = (hi - lo).astype(jnp.float32)[None, :, None]
    return total / count - uf


def _layer(x, norm_g, w_in, w_pool, pool_scale, w_out):
    b, s, _ = x.shape
    h = _rmsnorm(x, norm_g)
    proj = jnp.einsum('bsd,de->bse', h, w_in)
    q, k, v, gate_a, u_pool, gate_p = jnp.split(
        proj, [D_ATTN, 2 * D_ATTN, 3 * D_ATTN, 4 * D_ATTN, 4 * D_ATTN + D_POOL], axis=-1)
    q = _rope(q.reshape(b, s, N_HEADS, HEAD_DIM))
    k = _rope(k.reshape(b, s, N_HEADS, HEAD_DIM))
    v = v.reshape(b, s, N_HEADS, HEAD_DIM)
    attn = _dilated_attention_mixture(q, k, v).reshape(b, s, D_ATTN).astype(x.dtype)
    u = u_pool.reshape(b, s, N_POOL_GROUPS, POOL_GROUP_DIM)
    pooled = jnp.stack([_centred_mean_minus_identity(u[:, :, g], w)
                        for g, w in enumerate(POOL_WINDOWS)], axis=2)
    pool = jnp.einsum('bsgc,gcd->bsgd', pooled.astype(x.dtype), w_pool).reshape(b, s, D_POOL) * pool_scale
    y = jnp.concatenate([attn * jax.nn.silu(gate_a), pool * jax.nn.silu(gate_p)], axis=-1)
    return x + jnp.einsum('bse,ed->bsd', y, w_out)


def _trunk(x, norm_g, w_in, w_pool, pool_scale, w_out, final_norm_g):
    for i in range(DEPTH):
        x = _layer(x, norm_g[i], w_in[i], w_pool[i], pool_scale[i], w_out[i])
    return _rmsnorm(x, final_norm_g)


def setup_inputs(seed: int = 0) -> dict:
    key = jax.random.key(seed)
    ks = jax.random.split(key, 8)
    f32 = jnp.float32
    x_prompt = jax.random.normal(ks[0], (BATCH, SEQ, D_MODEL), f32)
    x_sample = jax.random.normal(ks[1], (DEC_BATCH, DEC_SEQ, D_MODEL), f32)
    norm_g = 1.0 + 0.05 * jax.random.normal(ks[2], (DEPTH, D_MODEL), f32)
    w_in = jax.random.normal(ks[3], (DEPTH, D_MODEL, D_IN), f32) * D_MODEL ** -0.5
    w_pool = jax.random.normal(ks[4], (DEPTH, N_POOL_GROUPS, POOL_GROUP_DIM, POOL_GROUP_DIM), f32) * POOL_GROUP_DIM ** -0.5
    pool_scale = 1.0 + 0.1 * jax.random.normal(ks[5], (DEPTH, D_POOL), f32)
    w_out = jax.random.normal(ks[6], (DEPTH, D_MIX, D_MODEL), f32) * D_MIX ** -0.5
    final_norm_g = 1.0 + 0.05 * jax.random.normal(ks[7], (D_MODEL,), f32)
    return {"x_prompt": x_prompt, "x_sample": x_sample, "norm_g": norm_g, "w_in": w_in,
            "w_pool": w_pool, "pool_scale": pool_scale, "w_out": w_out,
            "final_norm_g": final_norm_g}


def reference(x_prompt, x_sample, norm_g, w_in, w_pool, pool_scale, w_out, final_norm_g):
    y_prompt = _trunk(x_prompt, norm_g, w_in, w_pool, pool_scale, w_out, final_norm_g)
    y_sample = _trunk(x_sample, norm_g, w_in, w_pool, pool_scale, w_out, final_norm_g)
    return (y_prompt, y_sample)
```

```python
import functools

import numpy as np
import jax
import jax.numpy as jnp
from jax import lax
from jax.experimental import pallas as pl
from jax.experimental.pallas import tpu as pltpu

D_MODEL = 1024
D_ATTN = 1024
D_POOL = 1024
D_MIX = D_ATTN + D_POOL
D_IN = 4 * D_ATTN + 2 * D_POOL
HEAD_DIM = 64
N_HEADS = D_ATTN // HEAD_DIM
LANES = 128
N_HEAD_PAIRS = D_ATTN // LANES
DILATIONS = (1, 4, 16)
HALF = 64
POOL_WINDOWS = (2, 4, 8, 16)
POOL_GROUP_DIM = D_POOL // len(POOL_WINDOWS)
POOL_HALO = 16
ROPE_THETA = 10000.0
RMS_EPS = 1e-6

Q_BLOCK = 2 * HALF
K_BLOCK = Q_BLOCK + 2 * HALF
CHUNK = Q_BLOCK * DILATIONS[-1]
TOKEN_TILE = 256
MASK_VALUE = -1e30
VMEM_LIMIT = 56 * 1024 * 1024

F32 = jnp.float32
BF16 = jnp.bfloat16


def _rope_column_order():
    order = []
    half = HEAD_DIM // 2
    for pair in range(N_HEAD_PAIRS):
        h0, h1 = 2 * pair * HEAD_DIM, (2 * pair + 1) * HEAD_DIM
        order += list(range(h0, h0 + half)) + list(range(h1, h1 + half))
        order += list(range(h0 + half, h0 + HEAD_DIM)) + list(range(h1 + half, h1 + HEAD_DIM))
    return np.asarray(order, dtype=np.int32)


def _band_bias():
    qi = np.arange(Q_BLOCK)[:, None]
    kj = np.arange(K_BLOCK)[None, :]
    band = np.abs(kj - HALF - qi) <= HALF
    out = []
    for first in (False, True):
        for last in (False, True):
            valid = band.copy()
            if first:
                valid &= kj >= HALF
            if last:
                valid &= kj < HALF + Q_BLOCK
            out.append(np.where(valid, 0.0, MASK_VALUE))
    return np.stack(out).astype(np.float32)


def _silu(g):
    return g / (1.0 + jnp.exp(-g))


def _inproj_kernel(x_ref, g_ref, w_ref, cos_ref, sin_ref,
                   q1_ref, q4_ref, q16_ref, k1_ref, k4_ref, k16_ref, v1_ref, v4_ref, v16_ref,
                   ga_ref, u_ref, gp_ref, nat_ref, by4_ref):
    tm = TOKEN_TILE
    x = x_ref[0]
    inv = lax.rsqrt(jnp.mean(x * x, axis=-1, keepdims=True) + RMS_EPS)
    h = (x * inv * g_ref[...]).astype(BF16)
    cos = cos_ref[...]
    sin = sin_ref[...]

    def proj(section):
        return jnp.dot(h, w_ref[:, section * D_ATTN:(section + 1) * D_ATTN],
                       preferred_element_type=F32)

    def rope(t):
        return t * cos + pltpu.roll(t, LANES // 2, axis=1) * sin

    def store_views(slot, p, tile, out1, out4, out16):
        out1[0, p] = tile.astype(BF16)
        nat_ref[slot] = tile
        for b in range(4):
            rows = nat_ref[slot, pl.ds(b, tm // 4, stride=4), :]
            by4_ref[slot, b] = rows
            out4[0, p, :, b * LANES:(b + 1) * LANES] = rows.astype(BF16)
        for b in range(4):
            for a in range(4):
                r = 4 * a + b
                rows = by4_ref[slot, b, pl.ds(a, tm // 16, stride=4), :]
                out16[0, p, :, r * LANES:(r + 1) * LANES] = rows.astype(BF16)

    acc = proj(0)
    for p in range(N_HEAD_PAIRS):
        tile = rope(acc[:, p * LANES:(p + 1) * LANES]) * HEAD_DIM ** -0.5
        store_views(p, p, tile, q1_ref, q4_ref, q16_ref)
    acc = proj(1)
    for p in range(N_HEAD_PAIRS):
        tile = rope(acc[:, p * LANES:(p + 1) * LANES])
        store_views(N_HEAD_PAIRS + p, p, tile, k1_ref, k4_ref, k16_ref)
    acc = proj(2)
    for p in range(N_HEAD_PAIRS):
        store_views(2 * N_HEAD_PAIRS + p, p, acc[:, p * LANES:(p + 1) * LANES], v1_ref, v4_ref, v16_ref)
    acc = proj(3)
    for p in range(N_HEAD_PAIRS):
        ga_ref[0, p] = _silu(acc[:, p * LANES:(p + 1) * LANES]).astype(BF16)
    u_ref[0] = proj(4).astype(BF16)
    gp_ref[0] = _silu(proj(5)).astype(BF16)


def _inproj(x, g, w, cos, sin):
    b, s, _ = x.shape
    tm = TOKEN_TILE

    def pair_out(d):
        shape = jax.ShapeDtypeStruct((b, N_HEAD_PAIRS, s // d, LANES * d), BF16)
        spec = pl.BlockSpec((1, N_HEAD_PAIRS, tm // d, LANES * d), lambda i, t: (i, 0, t, 0))
        return shape, spec

    flat_shape = jax.ShapeDtypeStruct((b, s, D_POOL), BF16)
    flat_spec = pl.BlockSpec((1, tm, D_POOL), lambda i, t: (i, t, 0))
    outs = [pair_out(d) for _ in range(3) for d in DILATIONS] + [pair_out(1)]
    outs += [(flat_shape, flat_spec), (flat_shape, flat_spec)]
    n_slots = 3 * N_HEAD_PAIRS
    return pl.pallas_call(
        _inproj_kernel,
        grid=(b, s // tm),
        in_specs=[
            pl.BlockSpec((1, tm, D_MODEL), lambda i, t: (i, t, 0)),
            pl.BlockSpec((1, D_MODEL), lambda i, t: (0, 0)),
            pl.BlockSpec((D_MODEL, D_IN), lambda i, t: (0, 0), pipeline_mode=pl.Buffered(1)),
            pl.BlockSpec((tm, LANES), lambda i, t: (t, 0)),
            pl.BlockSpec((tm, LANES), lambda i, t: (t, 0)),
        ],
        out_specs=[spec for _, spec in outs],
        out_shape=[shape for shape, _ in outs],
        scratch_shapes=[pltpu.VMEM((n_slots, tm, LANES), F32),
                        pltpu.VMEM((n_slots, 4, tm // 4, LANES), F32)],
        compiler_params=pltpu.CompilerParams(
            dimension_semantics=("arbitrary", "arbitrary"), vmem_limit_bytes=VMEM_LIMIT),
        name="inproj",
    )(x, g, w, cos, sin)


def _merge(a, b):
    acc_a, m_a, l_a = a
    acc_b, m_b, l_b = b
    m = jnp.maximum(m_a, m_b)
    w_a = jnp.exp(m_a - m)
    w_b = jnp.exp(m_b - m)
    return w_a * acc_a + w_b * acc_b, m, w_a * l_a + w_b * l_b


def _attn_kernel(*refs, n_chunks):
    halo = n_chunks > 1
    it = iter(refs)
    q_refs = [next(it) for _ in DILATIONS]
    k_refs = [[next(it) for _ in range(3 if halo else 1)] for _ in DILATIONS]
    v_refs = [[next(it) for _ in range(3 if halo else 1)] for _ in DILATIONS]
    ga_ref = next(it)
    bias_ref = next(it)
    o_ref = next(it)
    acc16_ref, m16_ref, l16_ref, accn_ref, mn_ref, ln_ref = it

    ci = pl.program_id(1)
    first_chunk = (ci == 0).astype(jnp.int32)
    last_chunk = (ci == n_chunks - 1).astype(jnp.int32)

    lane = lax.broadcasted_iota(jnp.int32, (Q_BLOCK, LANES), 1)
    qk_head0 = (lane % (LANES // 2)) < HEAD_DIM // 2
    q_masks = (jnp.where(qk_head0, 1.0, 0.0).astype(BF16), jnp.where(qk_head0, 0.0, 1.0).astype(BF16))
    out_head0 = lane < HEAD_DIM
    ones = jnp.ones((K_BLOCK, LANES), BF16)

    def window(parts, lanes, rows, start):
        center = parts[0]
        zeros = jnp.zeros((HALF, LANES), BF16)
        prev = parts[1][0, 0, 0, :, lanes] if halo else zeros
        nxt = parts[2][0, 0, 0, :, lanes] if halo else zeros
        if rows == Q_BLOCK:
            return jnp.concatenate([prev, center[0, 0, :, lanes], nxt], axis=0)
        if isinstance(start, int) and start == 0:
            return jnp.concatenate([prev, center[0, 0, 0:K_BLOCK - HALF, lanes]], axis=0)
        if isinstance(start, int) and start == rows - Q_BLOCK:
            return jnp.concatenate([center[0, 0, rows - (K_BLOCK - HALF):rows, lanes], nxt], axis=0)
        return center[0, 0, pl.ds(pl.multiple_of(start - HALF, HALF), K_BLOCK), lanes]

    def block(view, residue, start, rows, bias_index):
        lanes = slice(residue * LANES, (residue + 1) * LANES)
        if isinstance(start, int):
            q = q_refs[view][0, 0, start:start + Q_BLOCK, lanes]
        else:
            q = q_refs[view][0, 0, pl.ds(pl.multiple_of(start, Q_BLOCK), Q_BLOCK), lanes]
        kwin = window(k_refs[view], lanes, rows, start)
        vwin = jnp.concatenate([window(v_refs[view], lanes, rows, start), ones], axis=1)
        bias = bias_ref[bias_index]
        per_head = []
        for h in range(2):
            s = lax.dot_general(q * q_masks[h], kwin, (((1,), (1,)), ((), ())),
                                preferred_element_type=F32) + bias
            m = jnp.max(s, axis=-1, keepdims=True)
            p = jnp.exp(s - m).astype(BF16)
            pv = jnp.dot(p, vwin, preferred_element_type=F32)
            per_head.append((pv[:, :LANES], jnp.broadcast_to(m, (Q_BLOCK, LANES)), pv[:, LANES:]))
        return tuple(jnp.where(out_head0, a, b) for a, b in zip(*per_head))

    def bias_index(is_first_block, is_last_block):
        first = first_chunk if is_first_block else 0
        last = last_chunk if is_last_block else 0
        return 2 * first + last

    rows16 = CHUNK // 16
    for r in range(16):
        a, b = divmod(r, 4)
        acc, m, l = block(2, r, 0, rows16, bias_index(True, True))
        acc16_ref[b, pl.ds(a, Q_BLOCK, stride=4), :] = acc
        m16_ref[b, pl.ds(a, Q_BLOCK, stride=4), :] = m
        l16_ref[b, pl.ds(a, Q_BLOCK, stride=4), :] = l

    rows4 = CHUNK // 4
    n4 = rows4 // Q_BLOCK
    for b in range(4):
        for j in range(n4):
            start = j * Q_BLOCK
            state = block(1, b, start, rows4, bias_index(j == 0, j == n4 - 1))
            prev = (acc16_ref[b, start:start + Q_BLOCK, :], m16_ref[b, start:start + Q_BLOCK, :],
                    l16_ref[b, start:start + Q_BLOCK, :])
            acc, m, l = _merge(state, prev)
            dst = pl.ds(4 * start + b, Q_BLOCK, stride=4)
            accn_ref[dst, :] = acc
            mn_ref[dst, :] = m
            ln_ref[dst, :] = l

    n1 = CHUNK // Q_BLOCK

    def finish(j, start, bias_idx):
        state = block(0, 0, start, CHUNK, bias_idx)
        if isinstance(start, int):
            rows = slice(start, start + Q_BLOCK)
        else:
            rows = pl.ds(pl.multiple_of(start, Q_BLOCK), Q_BLOCK)
        acc, _, l = _merge(state, (accn_ref[rows, :], mn_ref[rows, :], ln_ref[rows, :]))
        o_ref[0, 0, rows, :] = (acc / l * ga_ref[0, 0, rows, :].astype(F32)).astype(BF16)

    finish(0, 0, bias_index(True, False))

    def body(j, carry):
        finish(j, j * Q_BLOCK, 0)
        return carry

    lax.fori_loop(1, n1 - 1, body, 0)
    finish(n1 - 1, CHUNK - Q_BLOCK, bias_index(False, True))


def _attn(q_views, k_views, v_views, ga, bias):
    b, _, s, _ = q_views[0].shape
    n_chunks = s // CHUNK
    halo = n_chunks > 1
    args, specs = [], []

    def center(view, d):
        args.append(view)
        specs.append(pl.BlockSpec((1, 1, CHUNK // d, LANES * d), lambda i, c, p: (i, p, c, 0)))

    def halos(arr, d):
        blocks_per_chunk = CHUNK // d // HALF
        n_blocks = s // d // HALF
        view = arr.reshape(b, N_HEAD_PAIRS, n_blocks, HALF, LANES * d)
        shape = (1, 1, 1, HALF, LANES * d)
        args.append(view)
        specs.append(pl.BlockSpec(
            shape, lambda i, c, p: (i, p, jnp.maximum(c * blocks_per_chunk - 1, 0), 0, 0)))
        args.append(view)
        specs.append(pl.BlockSpec(
            shape, lambda i, c, p: (i, p, jnp.minimum((c + 1) * blocks_per_chunk, n_blocks - 1), 0, 0)))

    for view, d in zip(q_views, DILATIONS):
        center(view, d)
    for views in (k_views, v_views):
        for view, d in zip(views, DILATIONS):
            center(view, d)
            if halo:
                halos(view, d)
    args.append(ga)
    specs.append(pl.BlockSpec((1, 1, CHUNK, LANES), lambda i, c, p: (i, p, c, 0)))
    args.append(bias)
    specs.append(pl.BlockSpec(bias.shape, lambda i, c, p: (0, 0, 0)))

    state16 = pltpu.VMEM((4, CHUNK // 4, LANES), F32)
    state_n = pltpu.VMEM((CHUNK, LANES), F32)
    return pl.pallas_call(
        functools.partial(_attn_kernel, n_chunks=n_chunks),
        grid=(b, n_chunks, N_HEAD_PAIRS),
        in_specs=specs,
        out_specs=pl.BlockSpec((1, 1, CHUNK, LANES), lambda i, c, p: (i, p, c, 0)),
        out_shape=jax.ShapeDtypeStruct((b, N_HEAD_PAIRS, s, LANES), BF16),
        scratch_shapes=[state16, state16, state16, state_n, state_n, state_n],
        compiler_params=pltpu.CompilerParams(
            dimension_semantics=("arbitrary", "arbitrary", "arbitrary"), vmem_limit_bytes=VMEM_LIMIT),
        name="attn",
    )(*args)


def _outproj_kernel(ya_ref, u_ref, up_ref, un_ref, gp_ref, x_ref, wp_ref, ps_ref, wo_ref, fg_ref,
                    o_ref, uf_ref, *, seq_len, final):
    tm = TOKEN_TILE
    t = pl.program_id(1)
    n_t = pl.num_programs(1)
    uf_ref[0:POOL_HALO, :] = jnp.where(t > 0, up_ref[0, 0].astype(F32), 0.0)
    uf_ref[POOL_HALO:POOL_HALO + tm, :] = u_ref[0].astype(F32)
    uf_ref[POOL_HALO + tm:, :] = jnp.where(t < n_t - 1, un_ref[0, 0].astype(F32), 0.0)

    pos = t * tm + lax.broadcasted_iota(jnp.int32, (tm, POOL_GROUP_DIM), 0)
    parts = [ya_ref[0, p] for p in range(N_HEAD_PAIRS)]
    for g, w in enumerate(POOL_WINDOWS):
        lanes = slice(g * POOL_GROUP_DIM, (g + 1) * POOL_GROUP_DIM)
        total = uf_ref[POOL_HALO - w // 2:POOL_HALO - w // 2 + tm, lanes]
        for k in range(1 - w // 2, w // 2):
            total = total + uf_ref[POOL_HALO + k:POOL_HALO + k + tm, lanes]
        count = (jnp.minimum(pos + w // 2, seq_len) - jnp.maximum(pos - w // 2, 0)).astype(F32)
        pooled = total / count - uf_ref[POOL_HALO:POOL_HALO + tm, lanes]
        mixed = jnp.dot(pooled.astype(BF16), wp_ref[g], preferred_element_type=F32)
        parts.append((mixed * ps_ref[:, lanes] * gp_ref[0, :, lanes].astype(F32)).astype(BF16))
    y = jnp.concatenate(parts, axis=1)
    out = x_ref[0] + jnp.dot(y, wo_ref[...], preferred_element_type=F32)
    if final:
        inv = lax.rsqrt(jnp.mean(out * out, axis=-1, keepdims=True) + RMS_EPS)
        out = out * inv * fg_ref[...]
    o_ref[0] = out


def _outproj(ya, u, gp, x, wp, ps, wo, fg, final):
    b, s, _ = x.shape
    tm = TOKEN_TILE
    halo_blocks = s // POOL_HALO
    per_tile = tm // POOL_HALO
    u_halo = u.reshape(b, halo_blocks, POOL_HALO, D_POOL)
    const2 = lambda i, t: (0, 0)
    return pl.pallas_call(
        functools.partial(_outproj_kernel, seq_len=s, final=final),
        grid=(b, s // tm),
        in_specs=[
            pl.BlockSpec((1, N_HEAD_PAIRS, tm, LANES), lambda i, t: (i, 0, t, 0)),
            pl.BlockSpec((1, tm, D_POOL), lambda i, t: (i, t, 0)),
            pl.BlockSpec((1, 1, POOL_HALO, D_POOL),
                         lambda i, t: (i, jnp.maximum(t * per_tile - 1, 0), 0, 0)),
            pl.BlockSpec((1, 1, POOL_HALO, D_POOL),
                         lambda i, t: (i, jnp.minimum((t + 1) * per_tile, halo_blocks - 1), 0, 0)),
            pl.BlockSpec((1, tm, D_POOL), lambda i, t: (i, t, 0)),
            pl.BlockSpec((1, tm, D_MODEL), lambda i, t: (i, t, 0)),
            pl.BlockSpec(wp.shape, lambda i, t: (0, 0, 0)),
            pl.BlockSpec((1, D_POOL), const2),
            pl.BlockSpec((D_MIX, D_MODEL), const2),
            pl.BlockSpec((1, D_MODEL), const2),
        ],
        out_specs=pl.BlockSpec((1, tm, D_MODEL), lambda i, t: (i, t, 0)),
        out_shape=jax.ShapeDtypeStruct((b, s, D_MODEL), F32),
        scratch_shapes=[pltpu.VMEM((tm + 2 * POOL_HALO, D_POOL), F32)],
        compiler_params=pltpu.CompilerParams(
            dimension_semantics=("arbitrary", "arbitrary"), vmem_limit_bytes=VMEM_LIMIT),
        name="outproj",
    )(ya, u, u_halo, u_halo, gp, x, wp, ps, wo, fg)


def _rope_tables(seq_len):
    inv_freq = ROPE_THETA ** (-jnp.arange(0, HEAD_DIM, 2, dtype=F32) / HEAD_DIM)
    ang = jnp.arange(seq_len, dtype=F32)[:, None] * inv_freq[None, :]
    cos, sin = jnp.cos(ang), jnp.sin(ang)
    return (jnp.concatenate([cos, cos, cos, cos], axis=1),
            jnp.concatenate([-sin, -sin, sin, sin], axis=1))


def _trunk(x, norm_g, w_in, w_pool, pool_scale, w_out, final_norm_g, bias):
    depth = norm_g.shape[0]
    cos, sin = _rope_tables(x.shape[1])
    fg = final_norm_g.reshape(1, D_MODEL)
    for i in range(depth):
        *qkv, ga, u, gp = _inproj(x, norm_g[i].reshape(1, D_MODEL), w_in[i], cos, sin)
        ya = _attn(qkv[0:3], qkv[3:6], qkv[6:9], ga, bias)
        x = _outproj(ya, u, gp, x, w_pool[i], pool_scale[i].reshape(1, D_POOL), w_out[i], fg,
                     final=(i == depth - 1))
    return x


def kernel(x_prompt, x_sample, norm_g, w_in, w_pool, pool_scale, w_out, final_norm_g):
    order = _rope_column_order()
    w_in = jnp.concatenate(
        [w_in[:, :, :D_ATTN][:, :, order], w_in[:, :, D_ATTN:2 * D_ATTN][:, :, order],
         w_in[:, :, 2 * D_ATTN:]], axis=2).astype(BF16)
    w_pool = w_pool.astype(BF16)
    w_out = w_out.astype(BF16)
    bias = jnp.asarray(_band_bias())
    run = functools.partial(_trunk, norm_g=norm_g, w_in=w_in, w_pool=w_pool, pool_scale=pool_scale,
                            w_out=w_out, final_norm_g=final_norm_g, bias=bias)
    return (run(x_prompt), run(x_sample))
```

```python
import functools

import numpy as np
import jax
import jax.numpy as jnp
from jax import lax
from jax.experimental import pallas as pl
from jax.experimental.pallas import tpu as pltpu

D_MODEL = 1024
D_ATTN = 1024
D_POOL = 1024
D_MIX = D_ATTN + D_POOL
D_IN = 4 * D_ATTN + 2 * D_POOL
HEAD_DIM = 64
N_HEADS = D_ATTN // HEAD_DIM
LANES = 128
N_HEAD_PAIRS = D_ATTN // LANES
DILATIONS = (1, 4, 16)
HALF = 64
POOL_WINDOWS = (2, 4, 8, 16)
POOL_GROUP_DIM = D_POOL // len(POOL_WINDOWS)
POOL_HALO = 16
ROPE_THETA = 10000.0
RMS_EPS = 1e-6

Q_BLOCK = 2 * HALF
K_BLOCK = Q_BLOCK + 2 * HALF
CHUNK = Q_BLOCK * DILATIONS[-1]
TOKEN_TILE = 256
MASK_VALUE = -1e30
VMEM_LIMIT = 56 * 1024 * 1024

F32 = jnp.float32
BF16 = jnp.bfloat16


def _rope_column_order():
    order = []
    half = HEAD_DIM // 2
    for pair in range(N_HEAD_PAIRS):
        h0, h1 = 2 * pair * HEAD_DIM, (2 * pair + 1) * HEAD_DIM
        order += list(range(h0, h0 + half)) + list(range(h1, h1 + half))
        order += list(range(h0 + half, h0 + HEAD_DIM)) + list(range(h1 + half, h1 + HEAD_DIM))
    return np.asarray(order, dtype=np.int32)


def _band_bias():
    qi = np.arange(Q_BLOCK)[:, None]
    kj = np.arange(K_BLOCK)[None, :]
    band = np.abs(kj - HALF - qi) <= HALF
    out = []
    for first in (False, True):
        for last in (False, True):
            valid = band.copy()
            if first:
                valid &= kj >= HALF
            if last:
                valid &= kj < HALF + Q_BLOCK
            out.append(np.where(valid, 0.0, MASK_VALUE))
    return np.stack(out).astype(np.float32)


def _silu(g):
    return g / (1.0 + jnp.exp(-g))


def _inproj_kernel(x_ref, g_ref, w_ref, cos_ref, sin_ref,
                   q1_ref, q4_ref, q16_ref, k1_ref, k4_ref, k16_ref, v1_ref, v4_ref, v16_ref,
                   ga_ref, u_ref, gp_ref, nat_ref, by4_ref):
    tm = TOKEN_TILE
    x = x_ref[0]
    inv = lax.rsqrt(jnp.mean(x * x, axis=-1, keepdims=True) + RMS_EPS)
    h = (x * inv * g_ref[...]).astype(BF16)
    cos = cos_ref[...]
    sin = sin_ref[...]

    def proj(section):
        return jnp.dot(h, w_ref[:, section * D_ATTN:(section + 1) * D_ATTN],
                       preferred_element_type=F32)

    def rope(t):
        return t * cos + pltpu.roll(t, LANES // 2, axis=1) * sin

    def store_views(slot, p, tile, out1, out4, out16):
        out1[0, p] = tile.astype(BF16)
        nat_ref[slot] = tile
        for b in range(4):
            rows = nat_ref[slot, pl.ds(b, tm // 4, stride=4), :]
            by4_ref[slot, b] = rows
            out4[0, p, :, b * LANES:(b + 1) * LANES] = rows.astype(BF16)
        for b in range(4):
            for a in range(4):
                r = 4 * a + b
                rows = by4_ref[slot, b, pl.ds(a, tm // 16, stride=4), :]
                out16[0, p, :, r * LANES:(r + 1) * LANES] = rows.astype(BF16)

    acc = proj(0)
    for p in range(N_HEAD_PAIRS):
        tile = rope(acc[:, p * LANES:(p + 1) * LANES]) * HEAD_DIM ** -0.5
        store_views(p, p, tile, q1_ref, q4_ref, q16_ref)
    acc = proj(1)
    for p in range(N_HEAD_PAIRS):
        tile = rope(acc[:, p * LANES:(p + 1) * LANES])
        store_views(N_HEAD_PAIRS + p, p, tile, k1_ref, k4_ref, k16_ref)
    acc = proj(2)
    for p in range(N_HEAD_PAIRS):
        store_views(2 * N_HEAD_PAIRS + p, p, acc[:, p * LANES:(p + 1) * LANES], v1_ref, v4_ref, v16_ref)
    acc = proj(3)
    for p in range(N_HEAD_PAIRS):
        ga_ref[0, p] = _silu(acc[:, p * LANES:(p + 1) * LANES]).astype(BF16)
    u_ref[0] = proj(4).astype(BF16)
    gp_ref[0] = _silu(proj(5)).astype(BF16)


def _inproj(x, g, w, cos, sin):
    b, s, _ = x.shape
    tm = TOKEN_TILE

    def pair_out(d):
        shape = jax.ShapeDtypeStruct((b, N_HEAD_PAIRS, s // d, LANES * d), BF16)
        spec = pl.BlockSpec((1, N_HEAD_PAIRS, tm // d, LANES * d), lambda i, t: (i, 0, t, 0))
        return shape, spec

    flat_shape = jax.ShapeDtypeStruct((b, s, D_POOL), BF16)
    flat_spec = pl.BlockSpec((1, tm, D_POOL), lambda i, t: (i, t, 0))
    outs = [pair_out(d) for _ in range(3) for d in DILATIONS] + [pair_out(1)]
    outs += [(flat_shape, flat_spec), (flat_shape, flat_spec)]
    n_slots = 3 * N_HEAD_PAIRS
    return pl.pallas_call(
        _inproj_kernel,
        grid=(b, s // tm),
        in_specs=[
            pl.BlockSpec((1, tm, D_MODEL), lambda i, t: (i, t, 0)),
            pl.BlockSpec((1, D_MODEL), lambda i, t: (0, 0)),
            pl.BlockSpec((D_MODEL, D_IN), lambda i, t: (0, 0), pipeline_mode=pl.Buffered(1)),
            pl.BlockSpec((tm, LANES), lambda i, t: (t, 0)),
            pl.BlockSpec((tm, LANES), lambda i, t: (t, 0)),
        ],
        out_specs=[spec for _, spec in outs],
        out_shape=[shape for shape, _ in outs],
        scratch_shapes=[pltpu.VMEM((n_slots, tm, LANES), F32),
                        pltpu.VMEM((n_slots, 4, tm // 4, LANES), F32)],
        compiler_params=pltpu.CompilerParams(
            dimension_semantics=("arbitrary", "arbitrary"), vmem_limit_bytes=VMEM_LIMIT),
        name="inproj",
    )(x, g, w, cos, sin)


def _merge(a, b):
    acc_a, m_a, l_a = a
    acc_b, m_b, l_b = b
    m = jnp.maximum(m_a, m_b)
    w_a = jnp.exp(m_a - m)
    w_b = jnp.exp(m_b - m)
    return w_a * acc_a + w_b * acc_b, m, w_a * l_a + w_b * l_b


def _attn_kernel(*refs, n_chunks):
    halo = n_chunks > 1
    it = iter(refs)
    q_refs = [next(it) for _ in DILATIONS]
    k_refs = [[next(it) for _ in range(3 if halo else 1)] for _ in DILATIONS]
    v_refs = [[next(it) for _ in range(3 if halo else 1)] for _ in DILATIONS]
    ga_ref = next(it)
    bias_ref = next(it)
    if not halo:
        filler_ref = next(it)
        k_refs = [[parts[0], filler_ref, filler_ref] for parts in k_refs]
        v_refs = [[parts[0], filler_ref, filler_ref] for parts in v_refs]
    o_ref = next(it)
    acc16_ref, m16_ref, l16_ref, accn_ref, mn_ref, ln_ref = it

    ci = pl.program_id(1)
    first_chunk = (ci == 0).astype(jnp.int32)
    last_chunk = (ci == n_chunks - 1).astype(jnp.int32)

    lane = lax.broadcasted_iota(jnp.int32, (Q_BLOCK, LANES), 1)
    qk_head0 = (lane % (LANES // 2)) < HEAD_DIM // 2
    q_masks = (jnp.where(qk_head0, 1.0, 0.0).astype(BF16), jnp.where(qk_head0, 0.0, 1.0).astype(BF16))
    out_head0 = lane < HEAD_DIM
    ones = jnp.ones((K_BLOCK, LANES), BF16)

    def window(parts, lanes, rows, start):
        center = parts[0]
        prev = parts[1][0, 0, 0, :, lanes]
        nxt = parts[2][0, 0, 0, :, lanes]
        if rows == Q_BLOCK:
            return jnp.concatenate([prev, center[0, 0, :, lanes], nxt], axis=0)
        if start == 0:
            return jnp.concatenate([prev, center[0, 0, 0:K_BLOCK - HALF, lanes]], axis=0)
        if start == rows - Q_BLOCK:
            return jnp.concatenate([center[0, 0, rows - (K_BLOCK - HALF):rows, lanes], nxt], axis=0)
        return center[0, 0, start - HALF:start - HALF + K_BLOCK, lanes]

    def block(view, residue, start, rows, bias_index):
        lanes = slice(residue * LANES, (residue + 1) * LANES)
        q = q_refs[view][0, 0, start:start + Q_BLOCK, lanes]
        kwin = window(k_refs[view], lanes, rows, start)
        vwin = jnp.concatenate([window(v_refs[view], lanes, rows, start), ones], axis=1)
        bias = bias_ref[bias_index]
        per_head = []
        for h in range(2):
            s = lax.dot_general(q * q_masks[h], kwin, (((1,), (1,)), ((), ())),
                                preferred_element_type=F32) + bias
            m = jnp.max(s, axis=-1, keepdims=True)
            p = jnp.exp(s - m).astype(BF16)
            pv = jnp.dot(p, vwin, preferred_element_type=F32)
            per_head.append((pv[:, :LANES], jnp.broadcast_to(m, (Q_BLOCK, LANES)), pv[:, LANES:]))
        return tuple(jnp.where(out_head0, a, b) for a, b in zip(*per_head))

    def bias_index(is_first_block, is_last_block):
        first = first_chunk if is_first_block else 0
        last = last_chunk if is_last_block else 0
        return 2 * first + last

    rows16 = CHUNK // 16
    for r in range(16):
        a, b = divmod(r, 4)
        acc, m, l = block(2, r, 0, rows16, bias_index(True, True))
        acc16_ref[b, pl.ds(a, Q_BLOCK, stride=4), :] = acc
        m16_ref[b, pl.ds(a, Q_BLOCK, stride=4), :] = m
        l16_ref[b, pl.ds(a, Q_BLOCK, stride=4), :] = l

    rows4 = CHUNK // 4
    n4 = rows4 // Q_BLOCK
    for b in range(4):
        for j in range(n4):
            start = j * Q_BLOCK
            state = block(1, b, start, rows4, bias_index(j == 0, j == n4 - 1))
            prev = (acc16_ref[b, start:start + Q_BLOCK, :], m16_ref[b, start:start + Q_BLOCK, :],
                    l16_ref[b, start:start + Q_BLOCK, :])
            acc, m, l = _merge(state, prev)
            dst = pl.ds(4 * start + b, Q_BLOCK, stride=4)
            accn_ref[dst, :] = acc
            mn_ref[dst, :] = m
            ln_ref[dst, :] = l

    n1 = CHUNK // Q_BLOCK

    for j in range(n1):
        start = j * Q_BLOCK
        state = block(0, 0, start, CHUNK, bias_index(j == 0, j == n1 - 1))
        rows = slice(start, start + Q_BLOCK)
        acc, _, l = _merge(state, (accn_ref[rows, :], mn_ref[rows, :], ln_ref[rows, :]))
        o_ref[0, 0, rows, :] = (acc / l * ga_ref[0, 0, rows, :].astype(F32)).astype(BF16)


def _attn(q_views, k_views, v_views, ga, bias):
    b, _, s, _ = q_views[0].shape
    n_chunks = s // CHUNK
    halo = n_chunks > 1
    args, specs = [], []

    def center(view, d):
        args.append(view)
        specs.append(pl.BlockSpec((1, 1, CHUNK // d, LANES * d), lambda i, c, p: (i, p, c, 0)))

    def halos(arr, d):
        blocks_per_chunk = CHUNK // d // HALF
        n_blocks = s // d // HALF
        view = arr.reshape(b, N_HEAD_PAIRS, n_blocks, HALF, LANES * d)
        shape = (1, 1, 1, HALF, LANES * d)
        args.append(view)
        specs.append(pl.BlockSpec(
            shape, lambda i, c, p: (i, p, jnp.maximum(c * blocks_per_chunk - 1, 0), 0, 0)))
        args.append(view)
        specs.append(pl.BlockSpec(
            shape, lambda i, c, p: (i, p, jnp.minimum((c + 1) * blocks_per_chunk, n_blocks - 1), 0, 0)))

    for view, d in zip(q_views, DILATIONS):
        center(view, d)
    for views in (k_views, v_views):
        for view, d in zip(views, DILATIONS):
            center(view, d)
            if halo:
                halos(view, d)
    args.append(ga)
    specs.append(pl.BlockSpec((1, 1, CHUNK, LANES), lambda i, c, p: (i, p, c, 0)))
    args.append(bias)
    specs.append(pl.BlockSpec(bias.shape, lambda i, c, p: (0, 0, 0)))
    if not halo:
        filler_shape = (1, 1, 1, HALF, LANES * DILATIONS[-1])
        args.append(jnp.zeros(filler_shape, BF16))
        specs.append(pl.BlockSpec(filler_shape, lambda i, c, p: (0, 0, 0, 0, 0)))

    state16 = pltpu.VMEM((4, CHUNK // 4, LANES), F32)
    state_n = pltpu.VMEM((CHUNK, LANES), F32)
    return pl.pallas_call(
        functools.partial(_attn_kernel, n_chunks=n_chunks),
        grid=(b, n_chunks, N_HEAD_PAIRS),
        in_specs=specs,
        out_specs=pl.BlockSpec((1, 1, CHUNK, LANES), lambda i, c, p: (i, p, c, 0)),
        out_shape=jax.ShapeDtypeStruct((b, N_HEAD_PAIRS, s, LANES), BF16),
        scratch_shapes=[state16, state16, state16, state_n, state_n, state_n],
        compiler_params=pltpu.CompilerParams(
            dimension_semantics=("arbitrary", "arbitrary", "arbitrary"), vmem_limit_bytes=VMEM_LIMIT),
        name="attn",
    )(*args)


def _outproj_kernel(ya_ref, u_ref, up_ref, un_ref, gp_ref, x_ref, wp_ref, ps_ref, wo_ref, fg_ref,
                    o_ref, uf_ref, *, seq_len, final):
    tm = TOKEN_TILE
    t = pl.program_id(1)
    n_t = pl.num_programs(1)
    uf_ref[0:POOL_HALO, :] = jnp.where(t > 0, up_ref[0, 0].astype(F32), 0.0)
    uf_ref[POOL_HALO:POOL_HALO + tm, :] = u_ref[0].astype(F32)
    uf_ref[POOL_HALO + tm:, :] = jnp.where(t < n_t - 1, un_ref[0, 0].astype(F32), 0.0)

    pos = t * tm + lax.broadcasted_iota(jnp.int32, (tm, POOL_GROUP_DIM), 0)
    parts = [ya_ref[0, p] for p in range(N_HEAD_PAIRS)]
    for g, w in enumerate(POOL_WINDOWS):
        lanes = slice(g * POOL_GROUP_DIM, (g + 1) * POOL_GROUP_DIM)
        total = uf_ref[POOL_HALO - w // 2:POOL_HALO - w // 2 + tm, lanes]
        for k in range(1 - w // 2, w // 2):
            total = total + uf_ref[POOL_HALO + k:POOL_HALO + k + tm, lanes]
        count = (jnp.minimum(pos + w // 2, seq_len) - jnp.maximum(pos - w // 2, 0)).astype(F32)
        pooled = total / count - uf_ref[POOL_HALO:POOL_HALO + tm, lanes]
        mixed = jnp.dot(pooled.astype(BF16), wp_ref[g], preferred_element_type=F32)
        parts.append((mixed * ps_ref[:, lanes] * gp_ref[0, :, lanes].astype(F32)).astype(BF16))
    y = jnp.concatenate(parts, axis=1)
    out = x_ref[0] + jnp.dot(y, wo_ref[...], preferred_element_type=F32)
    if final:
        inv = lax.rsqrt(jnp.mean(out * out, axis=-1, keepdims=True) + RMS_EPS)
        out = out * inv * fg_ref[...]
    o_ref[0] = out


def _outproj(ya, u, gp, x, wp, ps, wo, fg, final):
    b, s, _ = x.shape
    tm = TOKEN_TILE
    halo_blocks = s // POOL_HALO
    per_tile = tm // POOL_HALO
    u_halo = u.reshape(b, halo_blocks, POOL_HALO, D_POOL)
    const2 = lambda i, t: (0, 0)
    return pl.pallas_call(
        functools.partial(_outproj_kernel, seq_len=s, final=final),
        grid=(b, s // tm),
        in_specs=[
            pl.BlockSpec((1, N_HEAD_PAIRS, tm, LANES), lambda i, t: (i, 0, t, 0)),
            pl.BlockSpec((1, tm, D_POOL), lambda i, t: (i, t, 0)),
            pl.BlockSpec((1, 1, POOL_HALO, D_POOL),
                         lambda i, t: (i, jnp.maximum(t * per_tile - 1, 0), 0, 0)),
            pl.BlockSpec((1, 1, POOL_HALO, D_POOL),
                         lambda i, t: (i, jnp.minimum((t + 1) * per_tile, halo_blocks - 1), 0, 0)),
            pl.BlockSpec((1, tm, D_POOL), lambda i, t: (i, t, 0)),
            pl.BlockSpec((1, tm, D_MODEL), lambda i, t: (i, t, 0)),
            pl.BlockSpec(wp.shape, lambda i, t: (0, 0, 0)),
            pl.BlockSpec((1, D_POOL), const2),
            pl.BlockSpec((D_MIX, D_MODEL), const2),
            pl.BlockSpec((1, D_MODEL), const2),
        ],
        out_specs=pl.BlockSpec((1, tm, D_MODEL), lambda i, t: (i, t, 0)),
        out_shape=jax.ShapeDtypeStruct((b, s, D_MODEL), F32),
        scratch_shapes=[pltpu.VMEM((tm + 2 * POOL_HALO, D_POOL), F32)],
        compiler_params=pltpu.CompilerParams(
            dimension_semantics=("arbitrary", "arbitrary"), vmem_limit_bytes=VMEM_LIMIT),
        name="outproj",
    )(ya, u, u_halo, u_halo, gp, x, wp, ps, wo, fg)


def _rope_tables(seq_len):
    inv_freq = ROPE_THETA ** (-jnp.arange(0, HEAD_DIM, 2, dtype=F32) / HEAD_DIM)
    ang = jnp.arange(seq_len, dtype=F32)[:, None] * inv_freq[None, :]
    cos, sin = jnp.cos(ang), jnp.sin(ang)
    return (jnp.concatenate([cos, cos, cos, cos], axis=1),
            jnp.concatenate([-sin, -sin, sin, sin], axis=1))


def _trunk(x, norm_g, w_in, w_pool, pool_scale, w_out, final_norm_g, bias):
    depth = norm_g.shape[0]
    cos, sin = _rope_tables(x.shape[1])
    fg = final_norm_g.reshape(1, D_MODEL)
    for i in range(depth):
        *qkv, ga, u, gp = _inproj(x, norm_g[i].reshape(1, D_MODEL), w_in[i], cos, sin)
        ya = _attn(qkv[0:3], qkv[3:6], qkv[6:9], ga, bias)
        x = _outproj(ya, u, gp, x, w_pool[i], pool_scale[i].reshape(1, D_POOL), w_out[i], fg,
                     final=(i == depth - 1))
    return x


def kernel(x_prompt, x_sample, norm_g, w_in, w_pool, pool_scale, w_out, final_norm_g):
    order = _rope_column_order()
    w_in = jnp.concatenate(
        [w_in[:, :, :D_ATTN][:, :, order], w_in[:, :, D_ATTN:2 * D_ATTN][:, :, order],
         w_in[:, :, 2 * D_ATTN:]], axis=2).astype(BF16)
    w_pool = w_pool.astype(BF16)
    w_out = w_out.astype(BF16)
    bias = jnp.asarray(_band_bias())
    run = functools.partial(_trunk, norm_g=norm_g, w_in=w_in, w_pool=w_pool, pool_scale=pool_scale,
                            w_out=w_out, final_norm_g=final_norm_g, bias=bias)
    return (run(x_prompt), run(x_sample))
```

```python
import functools

import numpy as np
import jax
import jax.numpy as jnp
from jax import lax
from jax.experimental import pallas as pl
from jax.experimental.pallas import tpu as pltpu

D_MODEL = 1024
D_ATTN = 1024
D_POOL = 1024
D_MIX = D_ATTN + D_POOL
D_IN = 4 * D_ATTN + 2 * D_POOL
HEAD_DIM = 64
N_HEADS = D_ATTN // HEAD_DIM
LANES = 128
N_HEAD_PAIRS = D_ATTN // LANES
DILATIONS = (1, 4, 16)
HALF = 64
POOL_WINDOWS = (2, 4, 8, 16)
POOL_GROUP_DIM = D_POOL // len(POOL_WINDOWS)
POOL_HALO = 16
POOL_BLOCK = 128
POOL_K = 256
ROPE_THETA = 10000.0
RMS_EPS = 1e-6

Q_BLOCK = 2 * HALF
K_BLOCK = Q_BLOCK + 2 * HALF
CHUNK = Q_BLOCK * DILATIONS[-1]
TOKEN_TILE = 512
SUB_TILE = 256
MASK_VALUE = -1e30
QUERY_SCALE = HEAD_DIM ** -0.5 * float(np.log2(np.e))
VMEM_LIMIT = 56 * 1024 * 1024

F32 = jnp.float32
BF16 = jnp.bfloat16


def _rope_column_order():
    order = []
    half = HEAD_DIM // 2
    for pair in range(N_HEAD_PAIRS):
        h0, h1 = 2 * pair * HEAD_DIM, (2 * pair + 1) * HEAD_DIM
        order += list(range(h0, h0 + half)) + list(range(h1, h1 + half))
        order += list(range(h0 + half, h0 + HEAD_DIM)) + list(range(h1 + half, h1 + HEAD_DIM))
    return np.asarray(order, dtype=np.int32)


def _band_bias():
    qi = np.arange(Q_BLOCK)[:, None]
    kj = np.arange(K_BLOCK)[None, :]
    band = np.abs(kj - HALF - qi) <= HALF
    out = []
    for first in (False, True):
        for last in (False, True):
            valid = band.copy()
            if first:
                valid &= kj >= HALF
            if last:
                valid &= kj < HALF + Q_BLOCK
            out.append(np.where(valid, 0.0, MASK_VALUE))
    return np.stack(out).astype(np.float32)


def _silu(g):
    return g / (1.0 + jnp.exp(-g))


def _inproj_kernel(x_ref, g_ref, w_ref, cos_ref, sin_ref,
                   q1_ref, q4_ref, q16_ref, k1_ref, k4_ref, k16_ref, v1_ref, v4_ref, v16_ref,
                   ga_ref, u_ref, gp_ref, nat_ref, by4_ref):
    for sub in range(TOKEN_TILE // SUB_TILE):
        _inproj_rows(sub, x_ref, g_ref, w_ref, cos_ref, sin_ref,
                     (q1_ref, q4_ref, q16_ref), (k1_ref, k4_ref, k16_ref), (v1_ref, v4_ref, v16_ref),
                     ga_ref, u_ref, gp_ref, nat_ref, by4_ref)


def _inproj_rows(sub, x_ref, g_ref, w_ref, cos_ref, sin_ref, q_refs, k_refs, v_refs,
                 ga_ref, u_ref, gp_ref, nat_ref, by4_ref):
    n = SUB_TILE
    rows = slice(sub * n, (sub + 1) * n)
    x = x_ref[0, rows]
    inv = lax.rsqrt(jnp.mean(x * x, axis=-1, keepdims=True) + RMS_EPS)
    h = (x * inv * g_ref[...]).astype(BF16)
    cos = cos_ref[rows]
    sin = sin_ref[rows]

    def proj(section):
        return jnp.dot(h, w_ref[:, section * D_ATTN:(section + 1) * D_ATTN],
                       preferred_element_type=F32)

    def rope(t):
        return t * cos + pltpu.roll(t, LANES // 2, axis=1) * sin

    def store_views(p, tile, outs):
        out1, out4, out16 = outs
        slot = sub * N_HEAD_PAIRS + p
        out1[0, p, rows] = tile.astype(BF16)
        nat_ref[slot] = tile
        for b in range(4):
            part = nat_ref[slot, pl.ds(b, n // 4, stride=4), :]
            by4_ref[slot, b] = part
            out4[0, p, sub * (n // 4):(sub + 1) * (n // 4), b * LANES:(b + 1) * LANES] = part.astype(BF16)
        for b in range(4):
            for a in range(4):
                r = 4 * a + b
                part = by4_ref[slot, b, pl.ds(a, n // 16, stride=4), :]
                out16[0, p, sub * (n // 16):(sub + 1) * (n // 16), r * LANES:(r + 1) * LANES] = (
                    part.astype(BF16))

    acc = proj(0)
    for p in range(N_HEAD_PAIRS):
        store_views(p, rope(acc[:, p * LANES:(p + 1) * LANES]) * QUERY_SCALE, q_refs)
    acc = proj(1)
    for p in range(N_HEAD_PAIRS):
        store_views(p, rope(acc[:, p * LANES:(p + 1) * LANES]), k_refs)
    acc = proj(2)
    for p in range(N_HEAD_PAIRS):
        store_views(p, acc[:, p * LANES:(p + 1) * LANES], v_refs)
    acc = proj(3)
    for p in range(N_HEAD_PAIRS):
        ga_ref[0, p, rows] = _silu(acc[:, p * LANES:(p + 1) * LANES]).astype(BF16)
    u_ref[0, rows] = proj(4).astype(BF16)
    gp_ref[0, rows] = _silu(proj(5)).astype(BF16)


def _inproj(x, g, w, cos, sin):
    b, s, _ = x.shape
    tm = TOKEN_TILE

    def pair_out(d):
        shape = jax.ShapeDtypeStruct((b, N_HEAD_PAIRS, s // d, LANES * d), BF16)
        spec = pl.BlockSpec((1, N_HEAD_PAIRS, tm // d, LANES * d), lambda i, t: (i, 0, t, 0))
        return shape, spec

    flat_shape = jax.ShapeDtypeStruct((b, s, D_POOL), BF16)
    flat_spec = pl.BlockSpec((1, tm, D_POOL), lambda i, t: (i, t, 0))
    outs = [pair_out(d) for _ in range(3) for d in DILATIONS] + [pair_out(1)]
    outs += [(flat_shape, flat_spec), (flat_shape, flat_spec)]
    n_slots = N_HEAD_PAIRS * (tm // SUB_TILE)
    return pl.pallas_call(
        _inproj_kernel,
        grid=(b, s // tm),
        in_specs=[
            pl.BlockSpec((1, tm, D_MODEL), lambda i, t: (i, t, 0)),
            pl.BlockSpec((1, D_MODEL), lambda i, t: (0, 0)),
            pl.BlockSpec((D_MODEL, D_IN), lambda i, t: (0, 0), pipeline_mode=pl.Buffered(1)),
            pl.BlockSpec((tm, LANES), lambda i, t: (t, 0)),
            pl.BlockSpec((tm, LANES), lambda i, t: (t, 0)),
        ],
        out_specs=[spec for _, spec in outs],
        out_shape=[shape for shape, _ in outs],
        scratch_shapes=[pltpu.VMEM((n_slots, SUB_TILE, LANES), F32),
                        pltpu.VMEM((n_slots, 4, SUB_TILE // 4, LANES), F32)],
        compiler_params=pltpu.CompilerParams(
            dimension_semantics=("arbitrary", "arbitrary"), vmem_limit_bytes=VMEM_LIMIT),
        name="inproj",
    )(x, g, w, cos, sin)


def _merge(a, b):
    acc_a, m_a, l_a = a
    acc_b, m_b, l_b = b
    m = jnp.maximum(m_a, m_b)
    w_a = jnp.exp2(m_a - m)
    w_b = jnp.exp2(m_b - m)
    return w_a * acc_a + w_b * acc_b, m, w_a * l_a + w_b * l_b


def _attn_kernel(*refs, n_chunks):
    halo = n_chunks > 1
    it = iter(refs)
    q_refs = [next(it) for _ in DILATIONS]
    k_refs = [[next(it) for _ in range(3 if halo else 1)] for _ in DILATIONS]
    v_refs = [[next(it) for _ in range(3 if halo else 1)] for _ in DILATIONS]
    ga_ref = next(it)
    bias_ref = next(it)
    if not halo:
        filler_ref = next(it)
        k_refs = [[parts[0], filler_ref, filler_ref] for parts in k_refs]
        v_refs = [[parts[0], filler_ref, filler_ref] for parts in v_refs]
    o_ref = next(it)
    acc16_ref, m16_ref, l16_ref, accn_ref, mn_ref, ln_ref = it

    ci = pl.program_id(1)
    first_chunk = (ci == 0).astype(jnp.int32)
    last_chunk = (ci == n_chunks - 1).astype(jnp.int32)

    lane = lax.broadcasted_iota(jnp.int32, (Q_BLOCK, LANES), 1)
    qk_head0 = (lane % (LANES // 2)) < HEAD_DIM // 2
    q_masks = (jnp.where(qk_head0, 1.0, 0.0).astype(BF16), jnp.where(qk_head0, 0.0, 1.0).astype(BF16))
    out_head0 = lane < HEAD_DIM
    ones = jnp.ones((K_BLOCK, LANES), BF16)

    def window(parts, lanes, rows, start):
        center = parts[0]
        prev = parts[1][0, 0, 0, :, lanes]
        nxt = parts[2][0, 0, 0, :, lanes]
        if rows == Q_BLOCK:
            return jnp.concatenate([prev, center[0, 0, :, lanes], nxt], axis=0)
        if start == 0:
            return jnp.concatenate([prev, center[0, 0, 0:K_BLOCK - HALF, lanes]], axis=0)
        if start == rows - Q_BLOCK:
            return jnp.concatenate([center[0, 0, rows - (K_BLOCK - HALF):rows, lanes], nxt], axis=0)
        return center[0, 0, start - HALF:start - HALF + K_BLOCK, lanes]

    def block(view, residue, start, rows, bias_index):
        lanes = slice(residue * LANES, (residue + 1) * LANES)
        q = q_refs[view][0, 0, start:start + Q_BLOCK, lanes]
        kwin = window(k_refs[view], lanes, rows, start)
        vwin = window(v_refs[view], lanes, rows, start)
        bias = bias_ref[bias_index]
        q2 = jnp.concatenate([q * q_masks[0], q * q_masks[1]], axis=0)
        s = lax.dot_general(q2, kwin, (((1,), (1,)), ((), ())), preferred_element_type=F32)
        ms, ps = [], []
        for h in range(2):
            sh = s[h * Q_BLOCK:(h + 1) * Q_BLOCK] + bias
            m = jnp.max(sh, axis=-1, keepdims=True)
            ps.append(jnp.exp2(sh - m).astype(BF16))
            ms.append(jnp.broadcast_to(m, (Q_BLOCK, LANES)))
        pv = jnp.dot(jnp.concatenate(ps, axis=0), jnp.concatenate([vwin, ones], axis=1),
                     preferred_element_type=F32)
        head0, head1 = pv[:Q_BLOCK], pv[Q_BLOCK:]
        return (jnp.where(out_head0, head0[:, :LANES], head1[:, :LANES]),
                jnp.where(out_head0, ms[0], ms[1]),
                jnp.where(out_head0, head0[:, LANES:], head1[:, LANES:]))

    def bias_index(is_first_block, is_last_block):
        first = first_chunk if is_first_block else 0
        last = last_chunk if is_last_block else 0
        return 2 * first + last

    rows16 = CHUNK // 16
    for r in range(16):
        a, b = divmod(r, 4)
        acc, m, l = block(2, r, 0, rows16, bias_index(True, True))
        acc16_ref[b, pl.ds(a, Q_BLOCK, stride=4), :] = acc
        m16_ref[b, pl.ds(a, Q_BLOCK, stride=4), :] = m
        l16_ref[b, pl.ds(a, Q_BLOCK, stride=4), :] = l

    rows4 = CHUNK // 4
    n4 = rows4 // Q_BLOCK
    for b in range(4):
        for j in range(n4):
            start = j * Q_BLOCK
            state = block(1, b, start, rows4, bias_index(j == 0, j == n4 - 1))
            prev = (acc16_ref[b, start:start + Q_BLOCK, :], m16_ref[b, start:start + Q_BLOCK, :],
                    l16_ref[b, start:start + Q_BLOCK, :])
            acc, m, l = _merge(state, prev)
            dst = pl.ds(4 * start + b, Q_BLOCK, stride=4)
            accn_ref[dst, :] = acc
            mn_ref[dst, :] = m
            ln_ref[dst, :] = l

    n1 = CHUNK // Q_BLOCK

    for j in range(n1):
        start = j * Q_BLOCK
        state = block(0, 0, start, CHUNK, bias_index(j == 0, j == n1 - 1))
        rows = slice(start, start + Q_BLOCK)
        acc, _, l = _merge(state, (accn_ref[rows, :], mn_ref[rows, :], ln_ref[rows, :]))
        o_ref[0, 0, rows, :] = (acc / l * ga_ref[0, 0, rows, :].astype(F32)).astype(BF16)


def _attn(q_views, k_views, v_views, ga, bias):
    b, _, s, _ = q_views[0].shape
    n_chunks = s // CHUNK
    halo = n_chunks > 1
    args, specs = [], []

    def center(view, d):
        args.append(view)
        specs.append(pl.BlockSpec((1, 1, CHUNK // d, LANES * d), lambda i, c, p: (i, p, c, 0)))

    def halos(arr, d):
        blocks_per_chunk = CHUNK // d // HALF
        n_blocks = s // d // HALF
        view = arr.reshape(b, N_HEAD_PAIRS, n_blocks, HALF, LANES * d)
        shape = (1, 1, 1, HALF, LANES * d)
        args.append(view)
        specs.append(pl.BlockSpec(
            shape, lambda i, c, p: (i, p, jnp.maximum(c * blocks_per_chunk - 1, 0), 0, 0)))
        args.append(view)
        specs.append(pl.BlockSpec(
            shape, lambda i, c, p: (i, p, jnp.minimum((c + 1) * blocks_per_chunk, n_blocks - 1), 0, 0)))

    for view, d in zip(q_views, DILATIONS):
        center(view, d)
    for views in (k_views, v_views):
        for view, d in zip(views, DILATIONS):
            center(view, d)
            if halo:
                halos(view, d)
    args.append(ga)
    specs.append(pl.BlockSpec((1, 1, CHUNK, LANES), lambda i, c, p: (i, p, c, 0)))
    args.append(bias)
    specs.append(pl.BlockSpec(bias.shape, lambda i, c, p: (0, 0, 0)))
    if not halo:
        filler_shape = (1, 1, 1, HALF, LANES * DILATIONS[-1])
        args.append(jnp.zeros(filler_shape, BF16))
        specs.append(pl.BlockSpec(filler_shape, lambda i, c, p: (0, 0, 0, 0, 0)))

    state16 = pltpu.VMEM((4, CHUNK // 4, LANES), F32)
    state_n = pltpu.VMEM((CHUNK, LANES), F32)
    return pl.pallas_call(
        functools.partial(_attn_kernel, n_chunks=n_chunks),
        grid=(b, n_chunks, N_HEAD_PAIRS),
        in_specs=specs,
        out_specs=pl.BlockSpec((1, 1, CHUNK, LANES), lambda i, c, p: (i, p, c, 0)),
        out_shape=jax.ShapeDtypeStruct((b, N_HEAD_PAIRS, s, LANES), BF16),
        scratch_shapes=[state16, state16, state16, state_n, state_n, state_n],
        compiler_params=pltpu.CompilerParams(
            dimension_semantics=("arbitrary", "arbitrary", "arbitrary"), vmem_limit_bytes=VMEM_LIMIT),
        name="attn",
    )(*args)


def _pool_bands():
    i = np.arange(POOL_BLOCK)[:, None]
    j = np.arange(POOL_K)[None, :]
    rel = j - POOL_HALO - i
    return np.stack([(rel >= -(w // 2)) & (rel <= w // 2 - 1) for w in POOL_WINDOWS]).astype(np.float32)


def _outproj_kernel(ya_ref, u_ref, up_ref, un_ref, gp_ref, x_ref, band_ref, wp_ref, ps_ref, wo_ref, fg_ref,
                    o_ref, *, seq_len, final):
    tm = TOKEN_TILE
    t = pl.program_id(1)
    n_t = pl.num_programs(1)
    halo_zero = jnp.zeros((POOL_HALO, D_POOL), BF16)
    ub = jnp.concatenate(
        [jnp.where(t > 0, up_ref[0, 0], halo_zero), u_ref[0], jnp.where(t < n_t - 1, un_ref[0, 0], halo_zero),
         jnp.zeros((POOL_K - POOL_BLOCK - 2 * POOL_HALO, D_POOL), BF16)], axis=0)

    pos = t * tm + lax.broadcasted_iota(jnp.int32, (tm, POOL_GROUP_DIM), 0)
    parts = [ya_ref[0, p] for p in range(N_HEAD_PAIRS)]
    for g, w in enumerate(POOL_WINDOWS):
        lanes = slice(g * POOL_GROUP_DIM, (g + 1) * POOL_GROUP_DIM)
        total = jnp.concatenate(
            [jnp.dot(band_ref[g], ub[r0:r0 + POOL_K, lanes], preferred_element_type=F32)
             for r0 in range(0, tm, POOL_BLOCK)], axis=0)
        count = (jnp.minimum(pos + w // 2, seq_len) - jnp.maximum(pos - w // 2, 0)).astype(F32)
        pooled = total / count - u_ref[0, :, lanes].astype(F32)
        mixed = jnp.dot(pooled.astype(BF16), wp_ref[g], preferred_element_type=F32)
        parts.append((mixed * ps_ref[:, lanes] * gp_ref[0, :, lanes].astype(F32)).astype(BF16))
    y = jnp.concatenate(parts, axis=1)
    out = x_ref[0] + jnp.dot(y, wo_ref[...], preferred_element_type=F32)
    if final:
        inv = lax.rsqrt(jnp.mean(out * out, axis=-1, keepdims=True) + RMS_EPS)
        out = out * inv * fg_ref[...]
    o_ref[0] = out


def _outproj(ya, u, gp, x, bands, wp, ps, wo, fg, final):
    b, s, _ = x.shape
    tm = TOKEN_TILE
    halo_blocks = s // POOL_HALO
    per_tile = tm // POOL_HALO
    u_halo = u.reshape(b, halo_blocks, POOL_HALO, D_POOL)
    const2 = lambda i, t: (0, 0)
    const3 = lambda i, t: (0, 0, 0)
    return pl.pallas_call(
        functools.partial(_outproj_kernel, seq_len=s, final=final),
        grid=(b, s // tm),
        in_specs=[
            pl.BlockSpec((1, N_HEAD_PAIRS, tm, LANES), lambda i, t: (i, 0, t, 0)),
            pl.BlockSpec((1, tm, D_POOL), lambda i, t: (i, t, 0)),
            pl.BlockSpec((1, 1, POOL_HALO, D_POOL),
                         lambda i, t: (i, jnp.maximum(t * per_tile - 1, 0), 0, 0)),
            pl.BlockSpec((1, 1, POOL_HALO, D_POOL),
                         lambda i, t: (i, jnp.minimum((t + 1) * per_tile, halo_blocks - 1), 0, 0)),
            pl.BlockSpec((1, tm, D_POOL), lambda i, t: (i, t, 0)),
            pl.BlockSpec((1, tm, D_MODEL), lambda i, t: (i, t, 0)),
            pl.BlockSpec(bands.shape, const3),
            pl.BlockSpec(wp.shape, const3),
            pl.BlockSpec((1, D_POOL), const2),
            pl.BlockSpec((D_MIX, D_MODEL), const2),
            pl.BlockSpec((1, D_MODEL), const2),
        ],
        out_specs=pl.BlockSpec((1, tm, D_MODEL), lambda i, t: (i, t, 0)),
        out_shape=jax.ShapeDtypeStruct((b, s, D_MODEL), F32),
        compiler_params=pltpu.CompilerParams(
            dimension_semantics=("arbitrary", "arbitrary"), vmem_limit_bytes=VMEM_LIMIT),
        name="outproj",
    )(ya, u, u_halo, u_halo, gp, x, bands, wp, ps, wo, fg)


def _rope_tables(seq_len):
    inv_freq = ROPE_THETA ** (-jnp.arange(0, HEAD_DIM, 2, dtype=F32) / HEAD_DIM)
    ang = jnp.arange(seq_len, dtype=F32)[:, None] * inv_freq[None, :]
    cos, sin = jnp.cos(ang), jnp.sin(ang)
    return (jnp.concatenate([cos, cos, cos, cos], axis=1),
            jnp.concatenate([-sin, -sin, sin, sin], axis=1))


def _trunk(x, norm_g, w_in, w_pool, pool_scale, w_out, final_norm_g, bias, bands):
    depth = norm_g.shape[0]
    cos, sin = _rope_tables(x.shape[1])
    fg = final_norm_g.reshape(1, D_MODEL)
    for i in range(depth):
        *qkv, ga, u, gp = _inproj(x, norm_g[i].reshape(1, D_MODEL), w_in[i], cos, sin)
        ya = _attn(qkv[0:3], qkv[3:6], qkv[6:9], ga, bias)
        x = _outproj(ya, u, gp, x, bands, w_pool[i], pool_scale[i].reshape(1, D_POOL), w_out[i], fg,
                     final=(i == depth - 1))
    return x


def kernel(x_prompt, x_sample, norm_g, w_in, w_pool, pool_scale, w_out, final_norm_g):
    order = _rope_column_order()
    w_in = jnp.concatenate(
        [w_in[:, :, :D_ATTN][:, :, order], w_in[:, :, D_ATTN:2 * D_ATTN][:, :, order],
         w_in[:, :, 2 * D_ATTN:]], axis=2).astype(BF16)
    w_pool = w_pool.astype(BF16)
    w_out = w_out.astype(BF16)
    bias = jnp.asarray(_band_bias())
    bands = jnp.asarray(_pool_bands(), dtype=BF16)
    run = functools.partial(_trunk, norm_g=norm_g, w_in=w_in, w_pool=w_pool, pool_scale=pool_scale,
                            w_out=w_out, final_norm_g=final_norm_g, bias=bias, bands=bands)
    return (run(x_prompt), run(x_sample))
```

```python
import functools

import numpy as np
import jax
import jax.numpy as jnp
from jax import lax
from jax.experimental import pallas as pl
from jax.experimental.pallas import tpu as pltpu

D_MODEL = 1024
D_ATTN = 1024
D_POOL = 1024
D_MIX = D_ATTN + D_POOL
D_IN = 4 * D_ATTN + 2 * D_POOL
HEAD_DIM = 64
N_HEADS = D_ATTN // HEAD_DIM
LANES = 128
N_HEAD_PAIRS = D_ATTN // LANES
DILATIONS = (1, 4, 16)
HALF = 64
POOL_WINDOWS = (2, 4, 8, 16)
POOL_GROUP_DIM = D_POOL // len(POOL_WINDOWS)
POOL_HALO = 16
POOL_BLOCK = 128
POOL_K = 256
ROPE_THETA = 10000.0
RMS_EPS = 1e-6

Q_BLOCK = 2 * HALF
K_BLOCK = Q_BLOCK + 2 * HALF
CHUNK = Q_BLOCK * DILATIONS[-1]
PAIRS_PER_STEP = 2
TOKEN_TILE = 512
SUB_TILE = 256
MASK_VALUE = -1e30
QUERY_SCALE = HEAD_DIM ** -0.5 * float(np.log2(np.e))
VMEM_LIMIT = 56 * 1024 * 1024

F32 = jnp.float32
BF16 = jnp.bfloat16


def _rope_column_order():
    order = []
    half = HEAD_DIM // 2
    for pair in range(N_HEAD_PAIRS):
        h0, h1 = 2 * pair * HEAD_DIM, (2 * pair + 1) * HEAD_DIM
        order += list(range(h0, h0 + half)) + list(range(h1, h1 + half))
        order += list(range(h0 + half, h0 + HEAD_DIM)) + list(range(h1 + half, h1 + HEAD_DIM))
    order = np.asarray(order, dtype=np.int32)
    return np.concatenate([order, D_ATTN + order, np.arange(2 * D_ATTN, D_IN, dtype=np.int32)])


def _band_bias():
    qi = np.arange(Q_BLOCK)[:, None]
    kj = np.arange(K_BLOCK)[None, :]
    band = np.abs(kj - HALF - qi) <= HALF
    out = []
    for first in (False, True):
        for last in (False, True):
            valid = band.copy()
            if first:
                valid &= kj >= HALF
            if last:
                valid &= kj < HALF + Q_BLOCK
            out.append(np.where(valid, 0.0, MASK_VALUE))
    return np.stack(out).astype(np.float32)


def _silu(g):
    return g / (1.0 + jnp.exp(-g))


def _inproj_kernel(x_ref, g_ref, w_ref, cos_ref, sin_ref,
                   q1_ref, q4_ref, q16_ref, k1_ref, k4_ref, k16_ref, v1_ref, v4_ref, v16_ref,
                   ga_ref, u_ref, gp_ref, nat_ref, by4_ref):
    for sub in range(TOKEN_TILE // SUB_TILE):
        _inproj_rows(sub, x_ref, g_ref, w_ref, cos_ref, sin_ref,
                     (q1_ref, q4_ref, q16_ref), (k1_ref, k4_ref, k16_ref), (v1_ref, v4_ref, v16_ref),
                     ga_ref, u_ref, gp_ref, nat_ref, by4_ref)


def _inproj_rows(sub, x_ref, g_ref, w_ref, cos_ref, sin_ref, q_refs, k_refs, v_refs,
                 ga_ref, u_ref, gp_ref, nat_ref, by4_ref):
    n = SUB_TILE
    rows = slice(sub * n, (sub + 1) * n)
    x = x_ref[0, rows]
    inv = lax.rsqrt(jnp.mean(x * x, axis=-1, keepdims=True) + RMS_EPS)
    h = (x * inv * g_ref[...]).astype(BF16)
    cos = cos_ref[rows]
    sin = sin_ref[rows]

    def proj(section):
        return jnp.dot(h, w_ref[:, section * D_ATTN:(section + 1) * D_ATTN],
                       preferred_element_type=F32)

    def rope(t):
        return t * cos + pltpu.roll(t, LANES // 2, axis=1) * sin

    def store_views(p, tile, outs):
        out1, out4, out16 = outs
        slot = sub * N_HEAD_PAIRS + p
        out1[0, p, rows] = tile.astype(BF16)
        nat_ref[slot] = tile
        for b in range(4):
            part = nat_ref[slot, pl.ds(b, n // 4, stride=4), :]
            by4_ref[slot, b] = part
            out4[0, p, sub * (n // 4):(sub + 1) * (n // 4), b * LANES:(b + 1) * LANES] = part.astype(BF16)
        for b in range(4):
            for a in range(4):
                r = 4 * a + b
                part = by4_ref[slot, b, pl.ds(a, n // 16, stride=4), :]
                out16[0, p, sub * (n // 16):(sub + 1) * (n // 16), r * LANES:(r + 1) * LANES] = (
                    part.astype(BF16))

    acc = proj(0)
    for p in range(N_HEAD_PAIRS):
        store_views(p, rope(acc[:, p * LANES:(p + 1) * LANES]) * QUERY_SCALE, q_refs)
    acc = proj(1)
    for p in range(N_HEAD_PAIRS):
        store_views(p, rope(acc[:, p * LANES:(p + 1) * LANES]), k_refs)
    acc = proj(2)
    for p in range(N_HEAD_PAIRS):
        store_views(p, acc[:, p * LANES:(p + 1) * LANES], v_refs)
    acc = proj(3)
    for p in range(N_HEAD_PAIRS):
        ga_ref[0, p, rows] = _silu(acc[:, p * LANES:(p + 1) * LANES]).astype(BF16)
    u_ref[0, rows] = proj(4).astype(BF16)
    gp_ref[0, rows] = _silu(proj(5)).astype(BF16)


def _inproj(x, g, w, cos, sin):
    b, s, _ = x.shape
    tm = TOKEN_TILE

    def pair_out(d):
        shape = jax.ShapeDtypeStruct((b, N_HEAD_PAIRS, s // d, LANES * d), BF16)
        spec = pl.BlockSpec((1, N_HEAD_PAIRS, tm // d, LANES * d), lambda i, t: (i, 0, t, 0))
        return shape, spec

    flat_shape = jax.ShapeDtypeStruct((b, s, D_POOL), BF16)
    flat_spec = pl.BlockSpec((1, tm, D_POOL), lambda i, t: (i, t, 0))
    outs = [pair_out(d) for _ in range(3) for d in DILATIONS] + [pair_out(1)]
    outs += [(flat_shape, flat_spec), (flat_shape, flat_spec)]
    n_slots = N_HEAD_PAIRS * (tm // SUB_TILE)
    return pl.pallas_call(
        _inproj_kernel,
        grid=(b, s // tm),
        in_specs=[
            pl.BlockSpec((1, tm, D_MODEL), lambda i, t: (i, t, 0)),
            pl.BlockSpec((1, D_MODEL), lambda i, t: (0, 0)),
            pl.BlockSpec((D_MODEL, D_IN), lambda i, t: (0, 0), pipeline_mode=pl.Buffered(1)),
            pl.BlockSpec((tm, LANES), lambda i, t: (t, 0)),
            pl.BlockSpec((tm, LANES), lambda i, t: (t, 0)),
        ],
        out_specs=[spec for _, spec in outs],
        out_shape=[shape for shape, _ in outs],
        scratch_shapes=[pltpu.VMEM((n_slots, SUB_TILE, LANES), F32),
                        pltpu.VMEM((n_slots, 4, SUB_TILE // 4, LANES), F32)],
        compiler_params=pltpu.CompilerParams(
            dimension_semantics=("arbitrary", "arbitrary"), vmem_limit_bytes=VMEM_LIMIT),
        name="inproj",
    )(x, g, w, cos, sin)


def _merge(a, b):
    acc_a, m_a, l_a = a
    acc_b, m_b, l_b = b
    m = jnp.maximum(m_a, m_b)
    w_a = jnp.exp2(m_a - m)
    w_b = jnp.exp2(m_b - m)
    return w_a * acc_a + w_b * acc_b, m, w_a * l_a + w_b * l_b


def _attn_kernel(*refs, n_chunks):
    halo = n_chunks > 1
    it = iter(refs)
    q_refs = [next(it) for _ in DILATIONS]
    k_refs = [[next(it) for _ in range(3 if halo else 1)] for _ in DILATIONS]
    v_refs = [[next(it) for _ in range(3 if halo else 1)] for _ in DILATIONS]
    ga_ref = next(it)
    bias_ref = next(it)
    if not halo:
        filler_ref = next(it)
        k_refs = [[parts[0], filler_ref, filler_ref] for parts in k_refs]
        v_refs = [[parts[0], filler_ref, filler_ref] for parts in v_refs]
    o_ref = next(it)
    scratch = tuple(it)

    ci = pl.program_id(1)
    first_chunk = (ci == 0).astype(jnp.int32)
    last_chunk = (ci == n_chunks - 1).astype(jnp.int32)

    word_lane = lax.broadcasted_iota(jnp.int32, (Q_BLOCK // 2, LANES), 1)
    all_bits = jnp.full((Q_BLOCK // 2, LANES), 0xFFFFFFFF, jnp.uint32)
    no_bits = jnp.zeros((Q_BLOCK // 2, LANES), jnp.uint32)
    in_head0 = (word_lane % (LANES // 2)) < HEAD_DIM // 2
    keep_head0 = jnp.where(in_head0, all_bits, no_bits)
    keep_head1 = jnp.where(in_head0, no_bits, all_bits)
    out_head0 = lax.broadcasted_iota(jnp.int32, (Q_BLOCK, LANES), 1) < HEAD_DIM
    ones = jnp.ones((K_BLOCK, LANES), BF16)

    def window(parts, hp, lanes, rows, start):
        center = parts[0]
        halo_hp = hp if halo else 0
        prev = parts[1][0, halo_hp, 0, :, lanes]
        nxt = parts[2][0, halo_hp, 0, :, lanes]
        if rows == Q_BLOCK:
            return jnp.concatenate([prev, center[0, hp, :, lanes], nxt], axis=0)
        if start == 0:
            return jnp.concatenate([prev, center[0, hp, 0:K_BLOCK - HALF, lanes]], axis=0)
        if start == rows - Q_BLOCK:
            return jnp.concatenate([center[0, hp, rows - (K_BLOCK - HALF):rows, lanes], nxt], axis=0)
        return center[0, hp, start - HALF:start - HALF + K_BLOCK, lanes]

    def block(hp, view, residue, start, rows, bias_index):
        lanes = slice(residue * LANES, (residue + 1) * LANES)
        q = pltpu.bitcast(q_refs[view][0, hp, start:start + Q_BLOCK, lanes], jnp.uint32)
        kwin = window(k_refs[view], hp, lanes, rows, start)
        vwin = window(v_refs[view], hp, lanes, rows, start)
        bias = bias_ref[bias_index]
        q2 = jnp.concatenate([pltpu.bitcast(q & keep_head0, BF16), pltpu.bitcast(q & keep_head1, BF16)], axis=0)
        s = lax.dot_general(q2, kwin, (((1,), (1,)), ((), ())), preferred_element_type=F32)
        ms, ps = [], []
        for h in range(2):
            sh = s[h * Q_BLOCK:(h + 1) * Q_BLOCK] + bias
            m = jnp.max(sh, axis=-1, keepdims=True)
            ps.append(jnp.exp2(sh - m).astype(BF16))
            ms.append(jnp.broadcast_to(m, (Q_BLOCK, LANES)))
        pv = jnp.dot(jnp.concatenate(ps, axis=0), jnp.concatenate([vwin, ones], axis=1),
                     preferred_element_type=F32)
        head0, head1 = pv[:Q_BLOCK], pv[Q_BLOCK:]
        return (jnp.where(out_head0, head0[:, :LANES], head1[:, :LANES]),
                jnp.where(out_head0, ms[0], ms[1]),
                jnp.where(out_head0, head0[:, LANES:], head1[:, LANES:]))

    def bias_index(is_first_block, is_last_block):
        first = first_chunk if is_first_block else 0
        last = last_chunk if is_last_block else 0
        return 2 * first + last

    rows16 = CHUNK // 16
    rows4 = CHUNK // 4
    n4 = rows4 // Q_BLOCK
    n1 = CHUNK // Q_BLOCK
    for hp in range(PAIRS_PER_STEP):
        acc16_ref, m16_ref, l16_ref, accn_ref, mn_ref, ln_ref = scratch[6 * hp:6 * hp + 6]

        for r in range(16):
            a, b = divmod(r, 4)
            acc, m, l = block(hp, 2, r, 0, rows16, bias_index(True, True))
            acc16_ref[b, pl.ds(a, Q_BLOCK, stride=4), :] = acc
            m16_ref[b, pl.ds(a, Q_BLOCK, stride=4), :] = m
            l16_ref[b, pl.ds(a, Q_BLOCK, stride=4), :] = l

        for b in range(4):
            for j in range(n4):
                start = j * Q_BLOCK
                state = block(hp, 1, b, start, rows4, bias_index(j == 0, j == n4 - 1))
                prev = (acc16_ref[b, start:start + Q_BLOCK, :], m16_ref[b, start:start + Q_BLOCK, :],
                        l16_ref[b, start:start + Q_BLOCK, :])
                acc, m, l = _merge(state, prev)
                dst = pl.ds(4 * start + b, Q_BLOCK, stride=4)
                accn_ref[dst, :] = acc
                mn_ref[dst, :] = m
                ln_ref[dst, :] = l

        for j in range(n1):
            start = j * Q_BLOCK
            state = block(hp, 0, 0, start, CHUNK, bias_index(j == 0, j == n1 - 1))
            rows = slice(start, start + Q_BLOCK)
            acc, _, l = _merge(state, (accn_ref[rows, :], mn_ref[rows, :], ln_ref[rows, :]))
            o_ref[0, hp, rows, :] = (acc / l * ga_ref[0, hp, rows, :].astype(F32)).astype(BF16)


def _attn(q_views, k_views, v_views, ga, bias):
    b, _, s, _ = q_views[0].shape
    n_chunks = s // CHUNK
    halo = n_chunks > 1
    args, specs = [], []

    pairs = PAIRS_PER_STEP

    def center(view, d):
        args.append(view)
        specs.append(pl.BlockSpec((1, pairs, CHUNK // d, LANES * d), lambda i, c, p: (i, p, c, 0)))

    def halos(arr, d):
        blocks_per_chunk = CHUNK // d // HALF
        n_blocks = s // d // HALF
        view = arr.reshape(b, N_HEAD_PAIRS, n_blocks, HALF, LANES * d)
        shape = (1, pairs, 1, HALF, LANES * d)
        args.append(view)
        specs.append(pl.BlockSpec(
            shape, lambda i, c, p: (i, p, jnp.maximum(c * blocks_per_chunk - 1, 0), 0, 0)))
        args.append(view)
        specs.append(pl.BlockSpec(
            shape, lambda i, c, p: (i, p, jnp.minimum((c + 1) * blocks_per_chunk, n_blocks - 1), 0, 0)))

    for view, d in zip(q_views, DILATIONS):
        center(view, d)
    for views in (k_views, v_views):
        for view, d in zip(views, DILATIONS):
            center(view, d)
            if halo:
                halos(view, d)
    args.append(ga)
    specs.append(pl.BlockSpec((1, pairs, CHUNK, LANES), lambda i, c, p: (i, p, c, 0)))
    args.append(bias)
    specs.append(pl.BlockSpec(bias.shape, lambda i, c, p: (0, 0, 0)))
    if not halo:
        filler_shape = (1, 1, 1, HALF, LANES * DILATIONS[-1])
        args.append(jnp.zeros(filler_shape, BF16))
        specs.append(pl.BlockSpec(filler_shape, lambda i, c, p: (0, 0, 0, 0, 0)))

    state16 = pltpu.VMEM((4, CHUNK // 4, LANES), F32)
    state_n = pltpu.VMEM((CHUNK, LANES), F32)
    return pl.pallas_call(
        functools.partial(_attn_kernel, n_chunks=n_chunks),
        grid=(b, n_chunks, N_HEAD_PAIRS // pairs),
        in_specs=specs,
        out_specs=pl.BlockSpec((1, pairs, CHUNK, LANES), lambda i, c, p: (i, p, c, 0)),
        out_shape=jax.ShapeDtypeStruct((b, N_HEAD_PAIRS, s, LANES), BF16),
        scratch_shapes=[state16, state16, state16, state_n, state_n, state_n] * pairs,
        compiler_params=pltpu.CompilerParams(
            dimension_semantics=("arbitrary", "arbitrary", "arbitrary"), vmem_limit_bytes=VMEM_LIMIT),
        name="attn",
    )(*args)


def _pool_bands():
    i = np.arange(POOL_BLOCK)[:, None]
    j = np.arange(POOL_K)[None, :]
    rel = j - POOL_HALO - i
    return np.stack([(rel >= -(w // 2)) & (rel <= w // 2 - 1) for w in POOL_WINDOWS]).astype(np.float32)


def _outproj_kernel(ya_ref, u_ref, up_ref, un_ref, gp_ref, x_ref, band_ref, wp_ref, ps_ref, wo_ref, fg_ref,
                    o_ref, *, seq_len, final):
    tm = TOKEN_TILE
    t = pl.program_id(1)
    n_t = pl.num_programs(1)
    halo_zero = jnp.zeros((POOL_HALO, D_POOL), BF16)
    ub = jnp.concatenate(
        [jnp.where(t > 0, up_ref[0, 0], halo_zero), u_ref[0], jnp.where(t < n_t - 1, un_ref[0, 0], halo_zero),
         jnp.zeros((POOL_K - POOL_BLOCK - 2 * POOL_HALO, D_POOL), BF16)], axis=0)

    pos = t * tm + lax.broadcasted_iota(jnp.int32, (tm, POOL_GROUP_DIM), 0)
    parts = [ya_ref[0, p] for p in range(N_HEAD_PAIRS)]
    for g, w in enumerate(POOL_WINDOWS):
        lanes = slice(g * POOL_GROUP_DIM, (g + 1) * POOL_GROUP_DIM)
        total = jnp.concatenate(
            [jnp.dot(band_ref[g], ub[r0:r0 + POOL_K, lanes], preferred_element_type=F32)
             for r0 in range(0, tm, POOL_BLOCK)], axis=0)
        count = (jnp.minimum(pos + w // 2, seq_len) - jnp.maximum(pos - w // 2, 0)).astype(F32)
        pooled = total / count - u_ref[0, :, lanes].astype(F32)
        mixed = jnp.dot(pooled.astype(BF16), wp_ref[g], preferred_element_type=F32)
        parts.append((mixed * ps_ref[:, lanes] * gp_ref[0, :, lanes].astype(F32)).astype(BF16))
    y = jnp.concatenate(parts, axis=1)
    out = x_ref[0] + jnp.dot(y, wo_ref[...], preferred_element_type=F32)
    if final:
        inv = lax.rsqrt(jnp.mean(out * out, axis=-1, keepdims=True) + RMS_EPS)
        out = out * inv * fg_ref[...]
    o_ref[0] = out


def _outproj(ya, u, gp, x, bands, wp, ps, wo, fg, final):
    b, s, _ = x.shape
    tm = TOKEN_TILE
    halo_blocks = s // POOL_HALO
    per_tile = tm // POOL_HALO
    u_halo = u.reshape(b, halo_blocks, POOL_HALO, D_POOL)
    const2 = lambda i, t: (0, 0)
    const3 = lambda i, t: (0, 0, 0)
    return pl.pallas_call(
        functools.partial(_outproj_kernel, seq_len=s, final=final),
        grid=(b, s // tm),
        in_specs=[
            pl.BlockSpec((1, N_HEAD_PAIRS, tm, LANES), lambda i, t: (i, 0, t, 0)),
            pl.BlockSpec((1, tm, D_POOL), lambda i, t: (i, t, 0)),
            pl.BlockSpec((1, 1, POOL_HALO, D_POOL),
                         lambda i, t: (i, jnp.maximum(t * per_tile - 1, 0), 0, 0)),
            pl.BlockSpec((1, 1, POOL_HALO, D_POOL),
                         lambda i, t: (i, jnp.minimum((t + 1) * per_tile, halo_blocks - 1), 0, 0)),
            pl.BlockSpec((1, tm, D_POOL), lambda i, t: (i, t, 0)),
            pl.BlockSpec((1, tm, D_MODEL), lambda i, t: (i, t, 0)),
            pl.BlockSpec(bands.shape, const3),
            pl.BlockSpec(wp.shape, const3),
            pl.BlockSpec((1, D_POOL), const2),
            pl.BlockSpec((D_MIX, D_MODEL), const2),
            pl.BlockSpec((1, D_MODEL), const2),
        ],
        out_specs=pl.BlockSpec((1, tm, D_MODEL), lambda i, t: (i, t, 0)),
        out_shape=jax.ShapeDtypeStruct((b, s, D_MODEL), F32),
        compiler_params=pltpu.CompilerParams(
            dimension_semantics=("arbitrary", "arbitrary"), vmem_limit_bytes=VMEM_LIMIT),
        name="outproj",
    )(ya, u, u_halo, u_halo, gp, x, bands, wp, ps, wo, fg)


def _rope_tables(seq_len):
    inv_freq = ROPE_THETA ** (-jnp.arange(0, HEAD_DIM, 2, dtype=F32) / HEAD_DIM)
    ang = jnp.arange(seq_len, dtype=F32)[:, None] * inv_freq[None, :]
    cos, sin = jnp.cos(ang), jnp.sin(ang)
    return (jnp.concatenate([cos, cos, cos, cos], axis=1),
            jnp.concatenate([-sin, -sin, sin, sin], axis=1))


def _trunk(x, norm_g, w_in, w_pool, pool_scale, w_out, final_norm_g, bias, bands):
    depth = norm_g.shape[0]
    cos, sin = _rope_tables(x.shape[1])
    fg = final_norm_g.reshape(1, D_MODEL)
    for i in range(depth):
        *qkv, ga, u, gp = _inproj(x, norm_g[i].reshape(1, D_MODEL), w_in[i], cos, sin)
        ya = _attn(qkv[0:3], qkv[3:6], qkv[6:9], ga, bias)
        x = _outproj(ya, u, gp, x, bands, w_pool[i], pool_scale[i].reshape(1, D_POOL), w_out[i], fg,
                     final=(i == depth - 1))
    return x


def kernel(x_prompt, x_sample, norm_g, w_in, w_pool, pool_scale, w_out, final_norm_g):
    w_in = jnp.take(w_in, _rope_column_order(), axis=2).astype(BF16)
    w_pool = w_pool.astype(BF16)
    w_out = w_out.astype(BF16)
    bias = jnp.asarray(_band_bias())
    bands = jnp.asarray(_pool_bands(), dtype=BF16)
    run = functools.partial(_trunk, norm_g=norm_g, w_in=w_in, w_pool=w_pool, pool_scale=pool_scale,
                            w_out=w_out, final_norm_g=final_norm_g, bias=bias, bands=bands)
    return (run(x_prompt), run(x_sample))
```

```python
import functools

import numpy as np
import jax
import jax.numpy as jnp
from jax import lax
from jax.experimental import pallas as pl
from jax.experimental.pallas import tpu as pltpu

D_MODEL = 1024
D_ATTN = 1024
D_POOL = 1024
D_MIX = D_ATTN + D_POOL
D_IN = 4 * D_ATTN + 2 * D_POOL
HEAD_DIM = 64
N_HEADS = D_ATTN // HEAD_DIM
LANES = 128
N_HEAD_PAIRS = D_ATTN // LANES
DILATIONS = (1, 4, 16)
VIEWS = (4, 16)
HALF = 64
POOL_WINDOWS = (2, 4, 8, 16)
POOL_GROUP_DIM = D_POOL // len(POOL_WINDOWS)
POOL_HALO = 16
POOL_BLOCK = 128
POOL_K = 256
ROPE_THETA = 10000.0
RMS_EPS = 1e-6

Q_BLOCK = 2 * HALF
K_BLOCK = Q_BLOCK + 2 * HALF
CHUNK = Q_BLOCK * DILATIONS[-1]
PAIRS_PER_STEP = 4
SCRATCH_SETS = 2
TOKEN_TILE = 512
OUT_TILE = 1024
SUB_TILE = 256
MASK_VALUE = -1e30
QUERY_SCALE = HEAD_DIM ** -0.5 * float(np.log2(np.e))
VMEM_LIMIT = 56 * 1024 * 1024

F32 = jnp.float32
BF16 = jnp.bfloat16


def _rope_column_order():
    order = []
    half = HEAD_DIM // 2
    for pair in range(N_HEAD_PAIRS):
        h0, h1 = 2 * pair * HEAD_DIM, (2 * pair + 1) * HEAD_DIM
        order += list(range(h0, h0 + half)) + list(range(h1, h1 + half))
        order += list(range(h0 + half, h0 + HEAD_DIM)) + list(range(h1 + half, h1 + HEAD_DIM))
    order = np.asarray(order, dtype=np.int32)
    return np.concatenate([order, D_ATTN + order, np.arange(2 * D_ATTN, D_IN, dtype=np.int32)])


GROUPED_BIAS = 4
GROUP_Q_ROWS = Q_BLOCK // 4
GROUP_K_ROWS = K_BLOCK // 4
GROUP_PAD = HALF // 4


def _band_bias():
    qi = np.arange(Q_BLOCK)[:, None]
    kj = np.arange(K_BLOCK)[None, :]
    out = []

    def variants(band, first_ok, last_ok):
        for first in (False, True):
            for last in (False, True):
                valid = band & (first_ok if first else True) & (last_ok if last else True)
                out.append(np.where(valid, 0.0, MASK_VALUE))

    variants(np.abs(kj - HALF - qi) <= HALF, kj >= HALF, kj < HALF + Q_BLOCK)
    qb, qw = qi // GROUP_Q_ROWS, qi % GROUP_Q_ROWS
    kb, kw = kj // GROUP_K_ROWS, kj % GROUP_K_ROWS
    rel = 4 * (kw - GROUP_PAD - qw) + (kb - qb)
    variants(np.abs(rel) <= HALF, kw >= GROUP_PAD, kw < GROUP_PAD + GROUP_Q_ROWS)
    return np.stack(out).astype(np.float32)


def _silu(g):
    return g / (1.0 + jnp.exp(-g))


def _inproj_kernel(x_ref, g_ref, w_ref, cos_ref, sin_ref,
                   q4_ref, q16_ref, k4_ref, k16_ref, v4_ref, v16_ref,
                   ga_ref, u_ref, gp_ref, nat_ref, by4_ref):
    for sub in range(TOKEN_TILE // SUB_TILE):
        _inproj_rows(sub, x_ref, g_ref, w_ref, cos_ref, sin_ref,
                     (q4_ref, q16_ref), (k4_ref, k16_ref), (v4_ref, v16_ref),
                     ga_ref, u_ref, gp_ref, nat_ref, by4_ref)


def _inproj_rows(sub, x_ref, g_ref, w_ref, cos_ref, sin_ref, q_refs, k_refs, v_refs,
                 ga_ref, u_ref, gp_ref, nat_ref, by4_ref):
    n = SUB_TILE
    rows = slice(sub * n, (sub + 1) * n)
    x = x_ref[0, rows]
    inv = lax.rsqrt(jnp.mean(x * x, axis=-1, keepdims=True) + RMS_EPS)
    h = (x * inv * g_ref[...]).astype(BF16)
    cos = cos_ref[rows]
    sin = sin_ref[rows]

    def proj(section):
        return jnp.dot(h, w_ref[:, section * D_ATTN:(section + 1) * D_ATTN],
                       preferred_element_type=F32)

    def rope(t):
        return t * cos + pltpu.roll(t, LANES // 2, axis=1) * sin

    def store_views(p, tile, outs):
        out4, out16 = outs
        slot = sub * N_HEAD_PAIRS + p
        nat_ref[slot] = tile
        for b in range(4):
            part = nat_ref[slot, pl.ds(b, n // 4, stride=4), :]
            by4_ref[slot, b] = part
            out4[0, p, sub * (n // 4):(sub + 1) * (n // 4), b * LANES:(b + 1) * LANES] = part.astype(BF16)
        for b in range(4):
            for a in range(4):
                r = 4 * a + b
                part = by4_ref[slot, b, pl.ds(a, n // 16, stride=4), :]
                out16[0, p, sub * (n // 16):(sub + 1) * (n // 16), r * LANES:(r + 1) * LANES] = (
                    part.astype(BF16))

    acc = proj(0)
    for p in range(N_HEAD_PAIRS):
        store_views(p, rope(acc[:, p * LANES:(p + 1) * LANES]) * QUERY_SCALE, q_refs)
    acc = proj(1)
    for p in range(N_HEAD_PAIRS):
        store_views(p, rope(acc[:, p * LANES:(p + 1) * LANES]), k_refs)
    acc = proj(2)
    for p in range(N_HEAD_PAIRS):
        store_views(p, acc[:, p * LANES:(p + 1) * LANES], v_refs)
    acc = proj(3)
    for p in range(N_HEAD_PAIRS):
        ga_ref[0, p, rows] = _silu(acc[:, p * LANES:(p + 1) * LANES]).astype(BF16)
    u_ref[0, rows] = proj(4).astype(BF16)
    gp_ref[0, rows] = _silu(proj(5)).astype(BF16)


def _inproj(x, g, w, cos, sin):
    b, s, _ = x.shape
    tm = TOKEN_TILE

    def pair_out(d):
        shape = jax.ShapeDtypeStruct((b, N_HEAD_PAIRS, s // d, LANES * d), BF16)
        spec = pl.BlockSpec((1, N_HEAD_PAIRS, tm // d, LANES * d), lambda i, t: (i, 0, t, 0))
        return shape, spec

    flat_shape = jax.ShapeDtypeStruct((b, s, D_POOL), BF16)
    flat_spec = pl.BlockSpec((1, tm, D_POOL), lambda i, t: (i, t, 0))
    outs = [pair_out(d) for _ in range(3) for d in VIEWS] + [pair_out(1)]
    outs += [(flat_shape, flat_spec), (flat_shape, flat_spec)]
    n_slots = N_HEAD_PAIRS * (tm // SUB_TILE)
    return pl.pallas_call(
        _inproj_kernel,
        grid=(b, s // tm),
        in_specs=[
            pl.BlockSpec((1, tm, D_MODEL), lambda i, t: (i, t, 0)),
            pl.BlockSpec((1, D_MODEL), lambda i, t: (0, 0)),
            pl.BlockSpec((D_MODEL, D_IN), lambda i, t: (0, 0), pipeline_mode=pl.Buffered(1)),
            pl.BlockSpec((tm, LANES), lambda i, t: (t, 0)),
            pl.BlockSpec((tm, LANES), lambda i, t: (t, 0)),
        ],
        out_specs=[spec for _, spec in outs],
        out_shape=[shape for shape, _ in outs],
        scratch_shapes=[pltpu.VMEM((n_slots, SUB_TILE, LANES), F32),
                        pltpu.VMEM((n_slots, 4, SUB_TILE // 4, LANES), F32)],
        compiler_params=pltpu.CompilerParams(
            dimension_semantics=("arbitrary", "arbitrary"), vmem_limit_bytes=VMEM_LIMIT),
        name="inproj",
    )(x, g, w, cos, sin)


def _merge(a, b):
    acc_a, m_a, l_a = a
    acc_b, m_b, l_b = b
    m = jnp.maximum(m_a, m_b)
    w_a = jnp.exp2(m_a - m)
    w_b = jnp.exp2(m_b - m)
    return w_a * acc_a + w_b * acc_b, m, w_a * l_a + w_b * l_b


def _attn_kernel(*refs, n_chunks):
    halo = n_chunks > 1
    it = iter(refs)
    q_refs = [next(it) for _ in VIEWS]
    k_refs = [[next(it) for _ in range(3 if halo else 1)] for _ in VIEWS]
    v_refs = [[next(it) for _ in range(3 if halo else 1)] for _ in VIEWS]
    ga_ref = next(it)
    bias_ref = next(it)
    if not halo:
        filler_ref = next(it)
        k_refs = [[parts[0], filler_ref, filler_ref] for parts in k_refs]
        v_refs = [[parts[0], filler_ref, filler_ref] for parts in v_refs]
    o_ref = next(it)
    scratch = tuple(it)

    ci = pl.program_id(1)
    first_chunk = (ci == 0).astype(jnp.int32)
    last_chunk = (ci == n_chunks - 1).astype(jnp.int32)

    word_lane = lax.broadcasted_iota(jnp.int32, (Q_BLOCK // 2, LANES), 1)
    all_bits = jnp.full((Q_BLOCK // 2, LANES), 0xFFFFFFFF, jnp.uint32)
    no_bits = jnp.zeros((Q_BLOCK // 2, LANES), jnp.uint32)
    in_head0 = (word_lane % (LANES // 2)) < HEAD_DIM // 2
    keep_head0 = jnp.where(in_head0, all_bits, no_bits)
    keep_head1 = jnp.where(in_head0, no_bits, all_bits)
    out_head0 = lax.broadcasted_iota(jnp.int32, (Q_BLOCK, LANES), 1) < HEAD_DIM
    ones = jnp.ones((K_BLOCK, LANES), BF16)

    def window(parts, hp, residue, lo, hi):
        lanes = slice(residue * LANES, (residue + 1) * LANES)
        rows = parts[0].shape[2]
        halo_hp = hp if halo else 0
        pieces = []
        if lo < 0:
            pieces.append(parts[1][0, halo_hp, 0, HALF + lo:HALF, lanes])
        pieces.append(parts[0][0, hp, max(lo, 0):min(hi, rows), lanes])
        if hi > rows:
            pieces.append(parts[2][0, halo_hp, 0, 0:hi - rows, lanes])
        return pieces

    def rows_of(pieces):
        return pieces[0] if len(pieces) == 1 else jnp.concatenate(pieces, axis=0)

    def block(q, kwin, vwin, bias_index):
        q = pltpu.bitcast(q, jnp.uint32)
        bias = bias_ref[bias_index]
        q2 = jnp.concatenate([pltpu.bitcast(q & keep_head0, BF16), pltpu.bitcast(q & keep_head1, BF16)], axis=0)
        s = lax.dot_general(q2, kwin, (((1,), (1,)), ((), ())), preferred_element_type=F32)
        ms, ps = [], []
        for h in range(2):
            sh = s[h * Q_BLOCK:(h + 1) * Q_BLOCK] + bias
            m = jnp.max(sh, axis=-1, keepdims=True)
            ps.append(jnp.exp2(sh - m).astype(BF16))
            ms.append(jnp.broadcast_to(m, (Q_BLOCK, LANES)))
        pv = jnp.dot(jnp.concatenate(ps, axis=0), jnp.concatenate([vwin, ones], axis=1),
                     preferred_element_type=F32)
        head0, head1 = pv[:Q_BLOCK], pv[Q_BLOCK:]
        return (jnp.where(out_head0, head0[:, :LANES], head1[:, :LANES]),
                jnp.where(out_head0, ms[0], ms[1]),
                jnp.where(out_head0, head0[:, LANES:], head1[:, LANES:]))

    def bias_index(base, is_first_block, is_last_block):
        first = first_chunk if is_first_block else 0
        last = last_chunk if is_last_block else 0
        return base + 2 * first + last

    q4_ref, q16_ref = q_refs
    (k4, k16), (v4, v16) = k_refs, v_refs
    rows16 = CHUNK // 16
    rows4 = CHUNK // 4
    for hp in range(PAIRS_PER_STEP):
        scratch_set = hp % SCRATCH_SETS
        acc_ref, m_ref, l_ref, tok_ref = scratch[4 * scratch_set:4 * scratch_set + 4]

        for r in range(16):
            a, b = divmod(r, 4)
            state = block(q16_ref[0, hp, :, r * LANES:(r + 1) * LANES],
                          rows_of(window(k16, hp, r, -HALF, rows16 + HALF)),
                          rows_of(window(v16, hp, r, -HALF, rows16 + HALF)),
                          bias_index(0, True, True))
            for ref, value in zip((acc_ref, m_ref, l_ref), state):
                ref[b, pl.ds(a, Q_BLOCK, stride=4), :] = value

        for b in range(4):
            for start in range(0, rows4, Q_BLOCK):
                rows = slice(start, start + Q_BLOCK)
                state = block(q4_ref[0, hp, rows, b * LANES:(b + 1) * LANES],
                              rows_of(window(k4, hp, b, start - HALF, start + Q_BLOCK + HALF)),
                              rows_of(window(v4, hp, b, start - HALF, start + Q_BLOCK + HALF)),
                              bias_index(0, start == 0, start == rows4 - Q_BLOCK))
                merged = _merge(state, (acc_ref[b, rows, :], m_ref[b, rows, :], l_ref[b, rows, :]))
                for ref, value in zip((acc_ref, m_ref, l_ref), merged):
                    ref[b, rows, :] = value

        for w0 in range(0, rows4, GROUP_Q_ROWS):
            lo, hi = w0 - GROUP_PAD, w0 - GROUP_PAD + GROUP_K_ROWS
            rows = slice(w0, w0 + GROUP_Q_ROWS)
            q = jnp.concatenate([q4_ref[0, hp, rows, b * LANES:(b + 1) * LANES] for b in range(4)], axis=0)
            kwin = jnp.concatenate([piece for b in range(4) for piece in window(k4, hp, b, lo, hi)], axis=0)
            vwin = jnp.concatenate([piece for b in range(4) for piece in window(v4, hp, b, lo, hi)], axis=0)
            state = block(q, kwin, vwin, bias_index(GROUPED_BIAS, w0 == 0, w0 == rows4 - GROUP_Q_ROWS))
            prev = tuple(jnp.concatenate([ref[b, rows, :] for b in range(4)], axis=0)
                         for ref in (acc_ref, m_ref, l_ref))
            acc, _, l = _merge(state, prev)
            out = acc / l
            tokens = slice(4 * w0, 4 * w0 + Q_BLOCK)
            for b in range(4):
                tok_ref[pl.ds(4 * w0 + b, GROUP_Q_ROWS, stride=4), :] = out[b * GROUP_Q_ROWS:(b + 1) * GROUP_Q_ROWS]
            o_ref[0, hp, tokens, :] = (tok_ref[tokens, :] * ga_ref[0, hp, tokens, :].astype(F32)).astype(BF16)


def _attn(q_views, k_views, v_views, ga, bias):
    b, _, rows, _ = q_views[0].shape
    s = rows * VIEWS[0]
    n_chunks = s // CHUNK
    halo = n_chunks > 1
    args, specs = [], []

    pairs = PAIRS_PER_STEP

    def center(view, d):
        args.append(view)
        specs.append(pl.BlockSpec((1, pairs, CHUNK // d, LANES * d), lambda i, c, p: (i, p, c, 0)))

    def halos(arr, d):
        blocks_per_chunk = CHUNK // d // HALF
        n_blocks = s // d // HALF
        view = arr.reshape(b, N_HEAD_PAIRS, n_blocks, HALF, LANES * d)
        shape = (1, pairs, 1, HALF, LANES * d)
        args.append(view)
        specs.append(pl.BlockSpec(
            shape, lambda i, c, p: (i, p, jnp.maximum(c * blocks_per_chunk - 1, 0), 0, 0)))
        args.append(view)
        specs.append(pl.BlockSpec(
            shape, lambda i, c, p: (i, p, jnp.minimum((c + 1) * blocks_per_chunk, n_blocks - 1), 0, 0)))

    for view, d in zip(q_views, VIEWS):
        center(view, d)
    for views in (k_views, v_views):
        for view, d in zip(views, VIEWS):
            center(view, d)
            if halo:
                halos(view, d)
    args.append(ga)
    specs.append(pl.BlockSpec((1, pairs, CHUNK, LANES), lambda i, c, p: (i, p, c, 0)))
    args.append(bias)
    specs.append(pl.BlockSpec(bias.shape, lambda i, c, p: (0, 0, 0)))
    if not halo:
        filler_shape = (1, 1, 1, HALF, LANES * VIEWS[-1])
        args.append(jnp.zeros(filler_shape, BF16))
        specs.append(pl.BlockSpec(filler_shape, lambda i, c, p: (0, 0, 0, 0, 0)))

    state = pltpu.VMEM((4, CHUNK // 4, LANES), F32)
    token_order = pltpu.VMEM((CHUNK, LANES), F32)
    return pl.pallas_call(
        functools.partial(_attn_kernel, n_chunks=n_chunks),
        grid=(b, n_chunks, N_HEAD_PAIRS // pairs),
        in_specs=specs,
        out_specs=pl.BlockSpec((1, pairs, CHUNK, LANES), lambda i, c, p: (i, p, c, 0)),
        out_shape=jax.ShapeDtypeStruct((b, N_HEAD_PAIRS, s, LANES), BF16),
        scratch_shapes=[state, state, state, token_order] * SCRATCH_SETS,
        compiler_params=pltpu.CompilerParams(
            dimension_semantics=("arbitrary", "arbitrary", "arbitrary"), vmem_limit_bytes=VMEM_LIMIT),
        name="attn",
    )(*args)


def _pool_bands():
    i = np.arange(POOL_BLOCK)[:, None]
    j = np.arange(POOL_K)[None, :]
    rel = j - POOL_HALO - i
    return np.stack([(rel >= -(w // 2)) & (rel <= w // 2 - 1) for w in POOL_WINDOWS]).astype(np.float32)


def _outproj_kernel(ya_ref, u_ref, up_ref, un_ref, gp_ref, x_ref, band_ref, wp_ref, ps_ref, wo_ref, fg_ref,
                    o_ref, *, seq_len, final):
    tm = OUT_TILE
    t = pl.program_id(1)
    n_t = pl.num_programs(1)
    halo_zero = jnp.zeros((POOL_HALO, D_POOL), BF16)
    ub = jnp.concatenate(
        [jnp.where(t > 0, up_ref[0, 0], halo_zero), u_ref[0], jnp.where(t < n_t - 1, un_ref[0, 0], halo_zero),
         jnp.zeros((POOL_K - POOL_BLOCK - 2 * POOL_HALO, D_POOL), BF16)], axis=0)

    pos = t * tm + lax.broadcasted_iota(jnp.int32, (tm, POOL_GROUP_DIM), 0)
    parts = [ya_ref[0, p] for p in range(N_HEAD_PAIRS)]
    for g, w in enumerate(POOL_WINDOWS):
        lanes = slice(g * POOL_GROUP_DIM, (g + 1) * POOL_GROUP_DIM)
        total = jnp.concatenate(
            [jnp.dot(band_ref[g], ub[r0:r0 + POOL_K, lanes], preferred_element_type=F32)
             for r0 in range(0, tm, POOL_BLOCK)], axis=0)
        count = (jnp.minimum(pos + w // 2, seq_len) - jnp.maximum(pos - w // 2, 0)).astype(F32)
        pooled = total / count - u_ref[0, :, lanes].astype(F32)
        mixed = jnp.dot(pooled.astype(BF16), wp_ref[g], preferred_element_type=F32)
        parts.append((mixed * ps_ref[:, lanes] * gp_ref[0, :, lanes].astype(F32)).astype(BF16))
    y = jnp.concatenate(parts, axis=1)
    out = x_ref[0] + jnp.dot(y, wo_ref[...], preferred_element_type=F32)
    if final:
        inv = lax.rsqrt(jnp.mean(out * out, axis=-1, keepdims=True) + RMS_EPS)
        out = out * inv * fg_ref[...]
    o_ref[0] = out


def _outproj(ya, u, gp, x, bands, wp, ps, wo, fg, final):
    b, s, _ = x.shape
    tm = OUT_TILE
    halo_blocks = s // POOL_HALO
    per_tile = tm // POOL_HALO
    u_halo = u.reshape(b, halo_blocks, POOL_HALO, D_POOL)
    const2 = lambda i, t: (0, 0)
    const3 = lambda i, t: (0, 0, 0)
    return pl.pallas_call(
        functools.partial(_outproj_kernel, seq_len=s, final=final),
        grid=(b, s // tm),
        in_specs=[
            pl.BlockSpec((1, N_HEAD_PAIRS, tm, LANES), lambda i, t: (i, 0, t, 0)),
            pl.BlockSpec((1, tm, D_POOL), lambda i, t: (i, t, 0)),
            pl.BlockSpec((1, 1, POOL_HALO, D_POOL),
                         lambda i, t: (i, jnp.maximum(t * per_tile - 1, 0), 0, 0)),
            pl.BlockSpec((1, 1, POOL_HALO, D_POOL),
                         lambda i, t: (i, jnp.minimum((t + 1) * per_tile, halo_blocks - 1), 0, 0)),
            pl.BlockSpec((1, tm, D_POOL), lambda i, t: (i, t, 0)),
            pl.BlockSpec((1, tm, D_MODEL), lambda i, t: (i, t, 0)),
            pl.BlockSpec(bands.shape, const3),
            pl.BlockSpec(wp.shape, const3),
            pl.BlockSpec((1, D_POOL), const2),
            pl.BlockSpec((D_MIX, D_MODEL), const2),
            pl.BlockSpec((1, D_MODEL), const2),
        ],
        out_specs=pl.BlockSpec((1, tm, D_MODEL), lambda i, t: (i, t, 0)),
        out_shape=jax.ShapeDtypeStruct((b, s, D_MODEL), F32),
        compiler_params=pltpu.CompilerParams(
            dimension_semantics=("arbitrary", "arbitrary"), vmem_limit_bytes=VMEM_LIMIT),
        name="outproj",
    )(ya, u, u_halo, u_halo, gp, x, bands, wp, ps, wo, fg)


def _rope_tables(seq_len):
    inv_freq = ROPE_THETA ** (-jnp.arange(0, HEAD_DIM, 2, dtype=F32) / HEAD_DIM)
    ang = jnp.arange(seq_len, dtype=F32)[:, None] * inv_freq[None, :]
    cos, sin = jnp.cos(ang), jnp.sin(ang)
    return (jnp.concatenate([cos, cos, cos, cos], axis=1),
            jnp.concatenate([-sin, -sin, sin, sin], axis=1))


def _trunk(x, norm_g, w_in, w_pool, pool_scale, w_out, final_norm_g, bias, bands):
    depth = norm_g.shape[0]
    cos, sin = _rope_tables(x.shape[1])
    fg = final_norm_g.reshape(1, D_MODEL)
    for i in range(depth):
        *qkv, ga, u, gp = _inproj(x, norm_g[i].reshape(1, D_MODEL), w_in[i], cos, sin)
        ya = _attn(qkv[0:2], qkv[2:4], qkv[4:6], ga, bias)
        x = _outproj(ya, u, gp, x, bands, w_pool[i], pool_scale[i].reshape(1, D_POOL), w_out[i], fg,
                     final=(i == depth - 1))
    return x


def kernel(x_prompt, x_sample, norm_g, w_in, w_pool, pool_scale, w_out, final_norm_g):
    w_in = jnp.take(w_in, _rope_column_order(), axis=2).astype(BF16)
    w_pool = w_pool.astype(BF16)
    w_out = w_out.astype(BF16)
    bias = jnp.asarray(_band_bias())
    bands = jnp.asarray(_pool_bands(), dtype=BF16)
    run = functools.partial(_trunk, norm_g=norm_g, w_in=w_in, w_pool=w_pool, pool_scale=pool_scale,
                            w_out=w_out, final_norm_g=final_norm_g, bias=bias, bands=bands)
    return (run(x_prompt), run(x_sample))
```

```python
import functools

import numpy as np
import jax
import jax.numpy as jnp
from jax import lax
from jax.experimental import pallas as pl
from jax.experimental.pallas import tpu as pltpu

D_MODEL = 1024
D_ATTN = 1024
D_POOL = 1024
D_MIX = D_ATTN + D_POOL
D_IN = 4 * D_ATTN + 2 * D_POOL
HEAD_DIM = 64
N_HEADS = D_ATTN // HEAD_DIM
LANES = 128
N_HEAD_PAIRS = D_ATTN // LANES
DILATIONS = (1, 4, 16)
VIEWS = (4, 16)
HALF = 64
POOL_WINDOWS = (2, 4, 8, 16)
POOL_GROUP_DIM = D_POOL // len(POOL_WINDOWS)
POOL_HALO = 16
POOL_BLOCK = 128
POOL_K = 256
ROPE_THETA = 10000.0
RMS_EPS = 1e-6

Q_BLOCK = 2 * HALF
K_BLOCK = Q_BLOCK + 2 * HALF
CHUNK = Q_BLOCK * DILATIONS[-1]
PAIRS_PER_STEP = 4
SCRATCH_SETS = 2
TOKEN_TILE = 512
OUT_TILE = 1024
SUB_TILE = 256
MASK_VALUE = -1e30
QUERY_SCALE = HEAD_DIM ** -0.5 * float(np.log2(np.e))
VMEM_LIMIT = 56 * 1024 * 1024

F32 = jnp.float32
BF16 = jnp.bfloat16


def _rope_column_order():
    order = []
    half = HEAD_DIM // 2
    for pair in range(N_HEAD_PAIRS):
        h0, h1 = 2 * pair * HEAD_DIM, (2 * pair + 1) * HEAD_DIM
        order += list(range(h0, h0 + half)) + list(range(h1, h1 + half))
        order += list(range(h0 + half, h0 + HEAD_DIM)) + list(range(h1 + half, h1 + HEAD_DIM))
    order = np.asarray(order, dtype=np.int32)
    return np.concatenate([order, D_ATTN + order, np.arange(2 * D_ATTN, D_IN, dtype=np.int32)])


GROUPED_BIAS = 4
GROUP_Q_ROWS = Q_BLOCK // 4
GROUP_K_ROWS = K_BLOCK // 4
GROUP_PAD = HALF // 4


def _band_bias():
    qi = np.arange(Q_BLOCK)[:, None]
    kj = np.arange(K_BLOCK)[None, :]
    out = []

    def variants(band, first_ok, last_ok):
        for first in (False, True):
            for last in (False, True):
                valid = band & (first_ok if first else True) & (last_ok if last else True)
                out.append(np.where(valid, 0.0, MASK_VALUE))

    variants(np.abs(kj - HALF - qi) <= HALF, kj >= HALF, kj < HALF + Q_BLOCK)
    qb, qw = qi // GROUP_Q_ROWS, qi % GROUP_Q_ROWS
    kb, kw = kj // GROUP_K_ROWS, kj % GROUP_K_ROWS
    rel = 4 * (kw - GROUP_PAD - qw) + (kb - qb)
    variants(np.abs(rel) <= HALF, kw >= GROUP_PAD, kw < GROUP_PAD + GROUP_Q_ROWS)
    return np.stack(out).astype(np.float32)


def _silu(g):
    return g / (1.0 + jnp.exp(-g))


def _inproj_kernel(x_ref, g_ref, w_ref, cos_ref, sin_ref,
                   q4_ref, q16_ref, k4_ref, k16_ref, v4_ref, v16_ref,
                   ga_ref, u_ref, gp_ref, nat_ref, by4_ref):
    for sub in range(TOKEN_TILE // SUB_TILE):
        _inproj_rows(sub, x_ref, g_ref, w_ref, cos_ref, sin_ref,
                     (q4_ref, q16_ref), (k4_ref, k16_ref), (v4_ref, v16_ref),
                     ga_ref, u_ref, gp_ref, nat_ref, by4_ref)


def _inproj_rows(sub, x_ref, g_ref, w_ref, cos_ref, sin_ref, q_refs, k_refs, v_refs,
                 ga_ref, u_ref, gp_ref, nat_ref, by4_ref):
    n = SUB_TILE
    rows = slice(sub * n, (sub + 1) * n)
    x = x_ref[0, rows]
    inv = lax.rsqrt(jnp.mean(x * x, axis=-1, keepdims=True) + RMS_EPS)
    h = (x * inv * g_ref[...]).astype(BF16)
    cos = cos_ref[rows]
    sin = sin_ref[rows]

    def proj(section):
        return jnp.dot(h, w_ref[:, section * D_ATTN:(section + 1) * D_ATTN],
                       preferred_element_type=F32)

    def rope(t):
        return t * cos + pltpu.roll(t, LANES // 2, axis=1) * sin

    def store_views(p, tile, outs):
        out4, out16 = outs
        slot = sub * N_HEAD_PAIRS + p
        nat_ref[slot] = tile
        for b in range(4):
            part = nat_ref[slot, pl.ds(b, n // 4, stride=4), :]
            by4_ref[slot, b * (n // 4):(b + 1) * (n // 4), :] = part
            out4[0, p, sub * (n // 4):(sub + 1) * (n // 4), b * LANES:(b + 1) * LANES] = part.astype(BF16)
        for a in range(4):
            stack = by4_ref[slot, pl.ds(a, n // 4, stride=4), :]
            for b in range(4):
                r = 4 * a + b
                out16[0, p, sub * (n // 16):(sub + 1) * (n // 16), r * LANES:(r + 1) * LANES] = (
                    stack[b * (n // 16):(b + 1) * (n // 16)].astype(BF16))

    acc = proj(0)
    for p in range(N_HEAD_PAIRS):
        store_views(p, rope(acc[:, p * LANES:(p + 1) * LANES]) * QUERY_SCALE, q_refs)
    acc = proj(1)
    for p in range(N_HEAD_PAIRS):
        store_views(p, rope(acc[:, p * LANES:(p + 1) * LANES]), k_refs)
    acc = proj(2)
    for p in range(N_HEAD_PAIRS):
        store_views(p, acc[:, p * LANES:(p + 1) * LANES], v_refs)
    acc = proj(3)
    for p in range(N_HEAD_PAIRS):
        ga_ref[0, p, rows] = _silu(acc[:, p * LANES:(p + 1) * LANES]).astype(BF16)
    u_ref[0, rows] = proj(4).astype(BF16)
    gp_ref[0, rows] = _silu(proj(5)).astype(BF16)


def _inproj(x, g, w, cos, sin):
    b, s, _ = x.shape
    tm = TOKEN_TILE

    def pair_out(d):
        shape = jax.ShapeDtypeStruct((b, N_HEAD_PAIRS, s // d, LANES * d), BF16)
        spec = pl.BlockSpec((1, N_HEAD_PAIRS, tm // d, LANES * d), lambda i, t: (i, 0, t, 0))
        return shape, spec

    flat_shape = jax.ShapeDtypeStruct((b, s, D_POOL), BF16)
    flat_spec = pl.BlockSpec((1, tm, D_POOL), lambda i, t: (i, t, 0))
    outs = [pair_out(d) for _ in range(3) for d in VIEWS] + [pair_out(1)]
    outs += [(flat_shape, flat_spec), (flat_shape, flat_spec)]
    n_slots = N_HEAD_PAIRS * (tm // SUB_TILE)
    return pl.pallas_call(
        _inproj_kernel,
        grid=(b, s // tm),
        in_specs=[
            pl.BlockSpec((1, tm, D_MODEL), lambda i, t: (i, t, 0)),
            pl.BlockSpec((1, D_MODEL), lambda i, t: (0, 0)),
            pl.BlockSpec((D_MODEL, D_IN), lambda i, t: (0, 0), pipeline_mode=pl.Buffered(1)),
            pl.BlockSpec((tm, LANES), lambda i, t: (t, 0)),
            pl.BlockSpec((tm, LANES), lambda i, t: (t, 0)),
        ],
        out_specs=[spec for _, spec in outs],
        out_shape=[shape for shape, _ in outs],
        scratch_shapes=[pltpu.VMEM((n_slots, SUB_TILE, LANES), F32),
                        pltpu.VMEM((n_slots, SUB_TILE, LANES), F32)],
        compiler_params=pltpu.CompilerParams(
            dimension_semantics=("arbitrary", "arbitrary"), vmem_limit_bytes=VMEM_LIMIT),
        name="inproj",
    )(x, g, w, cos, sin)


def _merge(a, b):
    acc_a, m_a, l_a = a
    acc_b, m_b, l_b = b
    m = jnp.maximum(m_a, m_b)
    w_a = jnp.exp2(m_a - m)
    w_b = jnp.exp2(m_b - m)
    return w_a * acc_a + w_b * acc_b, m, w_a * l_a + w_b * l_b


def _attn_kernel(*refs, n_chunks):
    halo = n_chunks > 1
    it = iter(refs)
    q_refs = [next(it) for _ in VIEWS]
    k_refs = [[next(it) for _ in range(3 if halo else 1)] for _ in VIEWS]
    v_refs = [[next(it) for _ in range(3 if halo else 1)] for _ in VIEWS]
    ga_ref = next(it)
    bias_ref = next(it)
    if not halo:
        filler_ref = next(it)
        k_refs = [[parts[0], filler_ref, filler_ref] for parts in k_refs]
        v_refs = [[parts[0], filler_ref, filler_ref] for parts in v_refs]
    o_ref = next(it)
    scratch = tuple(it)

    ci = pl.program_id(1)
    first_chunk = (ci == 0).astype(jnp.int32)
    last_chunk = (ci == n_chunks - 1).astype(jnp.int32)

    word_lane = lax.broadcasted_iota(jnp.int32, (Q_BLOCK // 2, LANES), 1)
    all_bits = jnp.full((Q_BLOCK // 2, LANES), 0xFFFFFFFF, jnp.uint32)
    no_bits = jnp.zeros((Q_BLOCK // 2, LANES), jnp.uint32)
    in_head0 = (word_lane % (LANES // 2)) < HEAD_DIM // 2
    keep_head0 = jnp.where(in_head0, all_bits, no_bits)
    keep_head1 = jnp.where(in_head0, no_bits, all_bits)
    out_head0 = lax.broadcasted_iota(jnp.int32, (Q_BLOCK, LANES), 1) < HEAD_DIM
    ones = jnp.ones((K_BLOCK, LANES), BF16)

    def window(parts, hp, residue, lo, hi):
        lanes = slice(residue * LANES, (residue + 1) * LANES)
        rows = parts[0].shape[2]
        halo_hp = hp if halo else 0
        pieces = []
        if lo < 0:
            pieces.append(parts[1][0, halo_hp, 0, HALF + lo:HALF, lanes])
        pieces.append(parts[0][0, hp, max(lo, 0):min(hi, rows), lanes])
        if hi > rows:
            pieces.append(parts[2][0, halo_hp, 0, 0:hi - rows, lanes])
        return pieces

    def rows_of(pieces):
        return pieces[0] if len(pieces) == 1 else jnp.concatenate(pieces, axis=0)

    def block(q, kwin, vwin, bias_index):
        q = pltpu.bitcast(q, jnp.uint32)
        bias = bias_ref[bias_index]
        q2 = jnp.concatenate([pltpu.bitcast(q & keep_head0, BF16), pltpu.bitcast(q & keep_head1, BF16)], axis=0)
        s = lax.dot_general(q2, kwin, (((1,), (1,)), ((), ())), preferred_element_type=F32)
        ms, ps = [], []
        for h in range(2):
            sh = s[h * Q_BLOCK:(h + 1) * Q_BLOCK] + bias
            m = jnp.max(sh, axis=-1, keepdims=True)
            ps.append(jnp.exp2(sh - m).astype(BF16))
            ms.append(jnp.broadcast_to(m, (Q_BLOCK, LANES)))
        pv = jnp.dot(jnp.concatenate(ps, axis=0), jnp.concatenate([vwin, ones], axis=1),
                     preferred_element_type=F32)
        head0, head1 = pv[:Q_BLOCK], pv[Q_BLOCK:]
        return (jnp.where(out_head0, head0[:, :LANES], head1[:, :LANES]),
                jnp.where(out_head0, ms[0], ms[1]),
                jnp.where(out_head0, head0[:, LANES:], head1[:, LANES:]))

    def bias_index(base, is_first_block, is_last_block):
        first = first_chunk if is_first_block else 0
        last = last_chunk if is_last_block else 0
        return base + 2 * first + last

    q4_ref, q16_ref = q_refs
    (k4, k16), (v4, v16) = k_refs, v_refs
    rows16 = CHUNK // 16
    rows4 = CHUNK // 4
    for hp in range(PAIRS_PER_STEP):
        scratch_set = hp % SCRATCH_SETS
        acc_ref, m_ref, l_ref, tok_ref = scratch[4 * scratch_set:4 * scratch_set + 4]

        for r in range(16):
            a, b = divmod(r, 4)
            state = block(q16_ref[0, hp, :, r * LANES:(r + 1) * LANES],
                          rows_of(window(k16, hp, r, -HALF, rows16 + HALF)),
                          rows_of(window(v16, hp, r, -HALF, rows16 + HALF)),
                          bias_index(0, True, True))
            for ref, value in zip((acc_ref, m_ref, l_ref), state):
                ref[b, pl.ds(a, Q_BLOCK, stride=4), :] = value

        for b in range(4):
            for start in range(0, rows4, Q_BLOCK):
                rows = slice(start, start + Q_BLOCK)
                state = block(q4_ref[0, hp, rows, b * LANES:(b + 1) * LANES],
                              rows_of(window(k4, hp, b, start - HALF, start + Q_BLOCK + HALF)),
                              rows_of(window(v4, hp, b, start - HALF, start + Q_BLOCK + HALF)),
                              bias_index(0, start == 0, start == rows4 - Q_BLOCK))
                merged = _merge(state, (acc_ref[b, rows, :], m_ref[b, rows, :], l_ref[b, rows, :]))
                for ref, value in zip((acc_ref, m_ref, l_ref), merged):
                    ref[b, rows, :] = value

        for w0 in range(0, rows4, GROUP_Q_ROWS):
            lo, hi = w0 - GROUP_PAD, w0 - GROUP_PAD + GROUP_K_ROWS
            rows = slice(w0, w0 + GROUP_Q_ROWS)
            q = jnp.concatenate([q4_ref[0, hp, rows, b * LANES:(b + 1) * LANES] for b in range(4)], axis=0)
            kwin = jnp.concatenate([piece for b in range(4) for piece in window(k4, hp, b, lo, hi)], axis=0)
            vwin = jnp.concatenate([piece for b in range(4) for piece in window(v4, hp, b, lo, hi)], axis=0)
            state = block(q, kwin, vwin, bias_index(GROUPED_BIAS, w0 == 0, w0 == rows4 - GROUP_Q_ROWS))
            prev = tuple(jnp.concatenate([ref[b, rows, :] for b in range(4)], axis=0)
                         for ref in (acc_ref, m_ref, l_ref))
            acc, _, l = _merge(state, prev)
            out = acc / l
            tokens = slice(4 * w0, 4 * w0 + Q_BLOCK)
            for b in range(4):
                tok_ref[pl.ds(4 * w0 + b, GROUP_Q_ROWS, stride=4), :] = out[b * GROUP_Q_ROWS:(b + 1) * GROUP_Q_ROWS]
            o_ref[0, hp, tokens, :] = (tok_ref[tokens, :] * ga_ref[0, hp, tokens, :].astype(F32)).astype(BF16)


def _attn(q_views, k_views, v_views, ga, bias):
    b, _, rows, _ = q_views[0].shape
    s = rows * VIEWS[0]
    n_chunks = s // CHUNK
    halo = n_chunks > 1
    args, specs = [], []

    pairs = PAIRS_PER_STEP

    def center(view, d):
        args.append(view)
        specs.append(pl.BlockSpec((1, pairs, CHUNK // d, LANES * d), lambda i, c, p: (i, p, c, 0)))

    def halos(arr, d):
        blocks_per_chunk = CHUNK // d // HALF
        n_blocks = s // d // HALF
        view = arr.reshape(b, N_HEAD_PAIRS, n_blocks, HALF, LANES * d)
        shape = (1, pairs, 1, HALF, LANES * d)
        args.append(view)
        specs.append(pl.BlockSpec(
            shape, lambda i, c, p: (i, p, jnp.maximum(c * blocks_per_chunk - 1, 0), 0, 0)))
        args.append(view)
        specs.append(pl.BlockSpec(
            shape, lambda i, c, p: (i, p, jnp.minimum((c + 1) * blocks_per_chunk, n_blocks - 1), 0, 0)))

    for view, d in zip(q_views, VIEWS):
        center(view, d)
    for views in (k_views, v_views):
        for view, d in zip(views, VIEWS):
            center(view, d)
            if halo:
                halos(view, d)
    args.append(ga)
    specs.append(pl.BlockSpec((1, pairs, CHUNK, LANES), lambda i, c, p: (i, p, c, 0)))
    args.append(bias)
    specs.append(pl.BlockSpec(bias.shape, lambda i, c, p: (0, 0, 0)))
    if not halo:
        filler_shape = (1, 1, 1, HALF, LANES * VIEWS[-1])
        args.append(jnp.zeros(filler_shape, BF16))
        specs.append(pl.BlockSpec(filler_shape, lambda i, c, p: (0, 0, 0, 0, 0)))

    state = pltpu.VMEM((4, CHUNK // 4, LANES), F32)
    token_order = pltpu.VMEM((CHUNK, LANES), F32)
    return pl.pallas_call(
        functools.partial(_attn_kernel, n_chunks=n_chunks),
        grid=(b, n_chunks, N_HEAD_PAIRS // pairs),
        in_specs=specs,
        out_specs=pl.BlockSpec((1, pairs, CHUNK, LANES), lambda i, c, p: (i, p, c, 0)),
        out_shape=jax.ShapeDtypeStruct((b, N_HEAD_PAIRS, s, LANES), BF16),
        scratch_shapes=[state, state, state, token_order] * SCRATCH_SETS,
        compiler_params=pltpu.CompilerParams(
            dimension_semantics=("arbitrary", "arbitrary", "arbitrary"), vmem_limit_bytes=VMEM_LIMIT),
        name="attn",
    )(*args)


def _pool_bands():
    i = np.arange(POOL_BLOCK)[:, None]
    j = np.arange(POOL_K)[None, :]
    rel = j - POOL_HALO - i
    return np.stack([(rel >= -(w // 2)) & (rel <= w // 2 - 1) for w in POOL_WINDOWS]).astype(np.float32)


def _outproj_kernel(ya_ref, u_ref, up_ref, un_ref, gp_ref, x_ref, band_ref, wp_ref, ps_ref, wo_ref, fg_ref,
                    o_ref, *, seq_len, final):
    tm = OUT_TILE
    t = pl.program_id(1)
    n_t = pl.num_programs(1)
    halo_zero = jnp.zeros((POOL_HALO, D_POOL), BF16)
    ub = jnp.concatenate(
        [jnp.where(t > 0, up_ref[0, 0], halo_zero), u_ref[0], jnp.where(t < n_t - 1, un_ref[0, 0], halo_zero),
         jnp.zeros((POOL_K - POOL_BLOCK - 2 * POOL_HALO, D_POOL), BF16)], axis=0)

    pos = t * tm + lax.broadcasted_iota(jnp.int32, (tm, POOL_GROUP_DIM), 0)
    parts = [ya_ref[0, p] for p in range(N_HEAD_PAIRS)]
    for g, w in enumerate(POOL_WINDOWS):
        lanes = slice(g * POOL_GROUP_DIM, (g + 1) * POOL_GROUP_DIM)
        total = jnp.concatenate(
            [jnp.dot(band_ref[g], ub[r0:r0 + POOL_K, lanes], preferred_element_type=F32)
             for r0 in range(0, tm, POOL_BLOCK)], axis=0)
        count = (jnp.minimum(pos + w // 2, seq_len) - jnp.maximum(pos - w // 2, 0)).astype(F32)
        pooled = total / count - u_ref[0, :, lanes].astype(F32)
        mixed = jnp.dot(pooled.astype(BF16), wp_ref[g], preferred_element_type=F32)
        parts.append((mixed * ps_ref[:, lanes] * gp_ref[0, :, lanes].astype(F32)).astype(BF16))
    y = jnp.concatenate(parts, axis=1)
    out = x_ref[0] + jnp.dot(y, wo_ref[...], preferred_element_type=F32)
    if final:
        inv = lax.rsqrt(jnp.mean(out * out, axis=-1, keepdims=True) + RMS_EPS)
        out = out * inv * fg_ref[...]
    o_ref[0] = out


def _outproj(ya, u, gp, x, bands, wp, ps, wo, fg, final):
    b, s, _ = x.shape
    tm = OUT_TILE
    halo_blocks = s // POOL_HALO
    per_tile = tm // POOL_HALO
    u_halo = u.reshape(b, halo_blocks, POOL_HALO, D_POOL)
    const2 = lambda i, t: (0, 0)
    const3 = lambda i, t: (0, 0, 0)
    return pl.pallas_call(
        functools.partial(_outproj_kernel, seq_len=s, final=final),
        grid=(b, s // tm),
        in_specs=[
            pl.BlockSpec((1, N_HEAD_PAIRS, tm, LANES), lambda i, t: (i, 0, t, 0)),
            pl.BlockSpec((1, tm, D_POOL), lambda i, t: (i, t, 0)),
            pl.BlockSpec((1, 1, POOL_HALO, D_POOL),
                         lambda i, t: (i, jnp.maximum(t * per_tile - 1, 0), 0, 0)),
            pl.BlockSpec((1, 1, POOL_HALO, D_POOL),
                         lambda i, t: (i, jnp.minimum((t + 1) * per_tile, halo_blocks - 1), 0, 0)),
            pl.BlockSpec((1, tm, D_POOL), lambda i, t: (i, t, 0)),
            pl.BlockSpec((1, tm, D_MODEL), lambda i, t: (i, t, 0)),
            pl.BlockSpec(bands.shape, const3),
            pl.BlockSpec(wp.shape, const3),
            pl.BlockSpec((1, D_POOL), const2),
            pl.BlockSpec((D_MIX, D_MODEL), const2),
            pl.BlockSpec((1, D_MODEL), const2),
        ],
        out_specs=pl.BlockSpec((1, tm, D_MODEL), lambda i, t: (i, t, 0)),
        out_shape=jax.ShapeDtypeStruct((b, s, D_MODEL), F32),
        compiler_params=pltpu.CompilerParams(
            dimension_semantics=("arbitrary", "arbitrary"), vmem_limit_bytes=VMEM_LIMIT),
        name="outproj",
    )(ya, u, u_halo, u_halo, gp, x, bands, wp, ps, wo, fg)


def _rope_tables(seq_len):
    inv_freq = ROPE_THETA ** (-jnp.arange(0, HEAD_DIM, 2, dtype=F32) / HEAD_DIM)
    ang = jnp.arange(seq_len, dtype=F32)[:, None] * inv_freq[None, :]
    cos, sin = jnp.cos(ang), jnp.sin(ang)
    return (jnp.concatenate([cos, cos, cos, cos], axis=1),
            jnp.concatenate([-sin, -sin, sin, sin], axis=1))


def _trunk(x, norm_g, w_in, w_pool, pool_scale, w_out, final_norm_g, bias, bands):
    depth = norm_g.shape[0]
    cos, sin = _rope_tables(x.shape[1])
    fg = final_norm_g.reshape(1, D_MODEL)
    for i in range(depth):
        *qkv, ga, u, gp = _inproj(x, norm_g[i].reshape(1, D_MODEL), w_in[i], cos, sin)
        ya = _attn(qkv[0:2], qkv[2:4], qkv[4:6], ga, bias)
        x = _outproj(ya, u, gp, x, bands, w_pool[i], pool_scale[i].reshape(1, D_POOL), w_out[i], fg,
                     final=(i == depth - 1))
    return x


def kernel(x_prompt, x_sample, norm_g, w_in, w_pool, pool_scale, w_out, final_norm_g):
    order = _rope_column_order()[:D_ATTN]
    w_in = jnp.concatenate(
        [w_in[:, :, :D_ATTN][:, :, order], w_in[:, :, D_ATTN:2 * D_ATTN][:, :, order],
         w_in[:, :, 2 * D_ATTN:]], axis=2).astype(BF16)
    w_pool = w_pool.astype(BF16)
    w_out = w_out.astype(BF16)
    bias = jnp.asarray(_band_bias())
    bands = jnp.asarray(_pool_bands(), dtype=BF16)
    run = functools.partial(_trunk, norm_g=norm_g, w_in=w_in, w_pool=w_pool, pool_scale=pool_scale,
                            w_out=w_out, final_norm_g=final_norm_g, bias=bias, bands=bands)
    return (run(x_prompt), run(x_sample))
```

```python
import functools

import numpy as np
import jax
import jax.numpy as jnp
from jax import lax
from jax.experimental import pallas as pl
from jax.experimental.pallas import tpu as pltpu

D_MODEL = 1024
D_ATTN = 1024
D_POOL = 1024
D_MIX = D_ATTN + D_POOL
D_IN = 4 * D_ATTN + 2 * D_POOL
HEAD_DIM = 64
N_HEADS = D_ATTN // HEAD_DIM
LANES = 128
N_HEAD_PAIRS = D_ATTN // LANES
DILATIONS = (1, 4, 16)
VIEWS = (4, 16)
HALF = 64
POOL_WINDOWS = (2, 4, 8, 16)
POOL_GROUP_DIM = D_POOL // len(POOL_WINDOWS)
POOL_HALO = 16
POOL_BLOCK = 128
POOL_K = 256
ROPE_THETA = 10000.0
RMS_EPS = 1e-6

Q_BLOCK = 2 * HALF
K_BLOCK = Q_BLOCK + 2 * HALF
CHUNK = Q_BLOCK * DILATIONS[-1]
PAIRS_PER_STEP = 4
SCRATCH_SETS = 2
TOKEN_TILE = 512
OUT_TILE = 1024
SUB_TILE = 256
MASK_VALUE = -1e30
QUERY_SCALE = HEAD_DIM ** -0.5 * float(np.log2(np.e))
VMEM_LIMIT = 56 * 1024 * 1024

F32 = jnp.float32
BF16 = jnp.bfloat16
HIGH_HALF = np.uint32(0xFFFF0000)


def _rope_column_order():
    order = []
    half = HEAD_DIM // 2
    for pair in range(N_HEAD_PAIRS):
        h0, h1 = 2 * pair * HEAD_DIM, (2 * pair + 1) * HEAD_DIM
        order += list(range(h0, h0 + half)) + list(range(h1, h1 + half))
        order += list(range(h0 + half, h0 + HEAD_DIM)) + list(range(h1 + half, h1 + HEAD_DIM))
    order = np.asarray(order, dtype=np.int32)
    return np.concatenate([order, D_ATTN + order, np.arange(2 * D_ATTN, D_IN, dtype=np.int32)])


GROUPED_BIAS = 4
GROUP_Q_ROWS = Q_BLOCK // 4
GROUP_K_ROWS = K_BLOCK // 4
GROUP_PAD = HALF // 4


def _band_bias():
    qi = np.arange(Q_BLOCK)[:, None]
    kj = np.arange(K_BLOCK)[None, :]
    out = []

    def variants(band, first_ok, last_ok):
        for first in (False, True):
            for last in (False, True):
                valid = band & (first_ok if first else True) & (last_ok if last else True)
                out.append(np.where(valid, 0.0, MASK_VALUE))

    variants(np.abs(kj - HALF - qi) <= HALF, kj >= HALF, kj < HALF + Q_BLOCK)
    qb, qw = qi // GROUP_Q_ROWS, qi % GROUP_Q_ROWS
    kb, kw = kj // GROUP_K_ROWS, kj % GROUP_K_ROWS
    rel = 4 * (kw - GROUP_PAD - qw) + (kb - qb)
    variants(np.abs(rel) <= HALF, kw >= GROUP_PAD, kw < GROUP_PAD + GROUP_Q_ROWS)
    return np.stack(out).astype(np.float32)


def _silu(g):
    return g / (1.0 + jnp.exp(-g))


def _inproj_kernel(x_ref, g_ref, w_ref, cos_ref, sin_ref,
                   q4_ref, q16_ref, k4_ref, k16_ref, v4_ref, v16_ref,
                   ga_ref, u_ref, gp_ref, nat_ref, by4_ref):
    for sub in range(TOKEN_TILE // SUB_TILE):
        _inproj_rows(sub, x_ref, g_ref, w_ref, cos_ref, sin_ref,
                     (q4_ref, q16_ref), (k4_ref, k16_ref), (v4_ref, v16_ref),
                     ga_ref, u_ref, gp_ref, nat_ref, by4_ref)


def _inproj_rows(sub, x_ref, g_ref, w_ref, cos_ref, sin_ref, q_refs, k_refs, v_refs,
                 ga_ref, u_ref, gp_ref, nat_ref, by4_ref):
    n = SUB_TILE
    rows = slice(sub * n, (sub + 1) * n)
    x = x_ref[0, rows]
    inv = lax.rsqrt(jnp.mean(x * x, axis=-1, keepdims=True) + RMS_EPS)
    h = (x * inv * g_ref[...]).astype(BF16)
    cos = cos_ref[rows]
    sin = sin_ref[rows]

    def proj(section):
        return jnp.dot(h, w_ref[:, section * D_ATTN:(section + 1) * D_ATTN],
                       preferred_element_type=F32)

    def rope(t):
        return t * cos + pltpu.roll(t, LANES // 2, axis=1) * sin

    def store_views(p, tiles, outs):
        out4, out16 = outs
        slot = sub * (N_HEAD_PAIRS // 2) + p // 2
        low, high = (pltpu.bitcast(t.astype(BF16).astype(F32), jnp.uint32) for t in tiles)
        nat_ref[slot] = (high & HIGH_HALF) | (low >> 16)

        def halves(words):
            return [pltpu.bitcast(words << 16, F32).astype(BF16),
                    pltpu.bitcast(words & HIGH_HALF, F32).astype(BF16)]

        for b in range(4):
            part = nat_ref[slot, pl.ds(b, n // 4, stride=4), :]
            by4_ref[slot, b * (n // 4):(b + 1) * (n // 4), :] = part
            for i, half in enumerate(halves(part)):
                out4[0, p + i, sub * (n // 4):(sub + 1) * (n // 4), b * LANES:(b + 1) * LANES] = half
        for a in range(4):
            stack = by4_ref[slot, pl.ds(a, n // 4, stride=4), :]
            for i, half in enumerate(halves(stack)):
                for b in range(4):
                    r = 4 * a + b
                    out16[0, p + i, sub * (n // 16):(sub + 1) * (n // 16), r * LANES:(r + 1) * LANES] = (
                        half[b * (n // 16):(b + 1) * (n // 16)])

    def pair_tiles(acc, p, finish):
        return [finish(acc[:, q * LANES:(q + 1) * LANES]) for q in (p, p + 1)]

    acc = proj(0)
    for p in range(0, N_HEAD_PAIRS, 2):
        store_views(p, pair_tiles(acc, p, lambda t: rope(t) * QUERY_SCALE), q_refs)
    acc = proj(1)
    for p in range(0, N_HEAD_PAIRS, 2):
        store_views(p, pair_tiles(acc, p, rope), k_refs)
    acc = proj(2)
    for p in range(0, N_HEAD_PAIRS, 2):
        store_views(p, pair_tiles(acc, p, lambda t: t), v_refs)
    acc = proj(3)
    for p in range(N_HEAD_PAIRS):
        ga_ref[0, p, rows] = _silu(acc[:, p * LANES:(p + 1) * LANES]).astype(BF16)
    u_ref[0, rows] = proj(4).astype(BF16)
    gp_ref[0, rows] = _silu(proj(5)).astype(BF16)


def _inproj(x, g, w, cos, sin):
    b, s, _ = x.shape
    tm = TOKEN_TILE

    def pair_out(d):
        shape = jax.ShapeDtypeStruct((b, N_HEAD_PAIRS, s // d, LANES * d), BF16)
        spec = pl.BlockSpec((1, N_HEAD_PAIRS, tm // d, LANES * d), lambda i, t: (i, 0, t, 0))
        return shape, spec

    flat_shape = jax.ShapeDtypeStruct((b, s, D_POOL), BF16)
    flat_spec = pl.BlockSpec((1, tm, D_POOL), lambda i, t: (i, t, 0))
    outs = [pair_out(d) for _ in range(3) for d in VIEWS] + [pair_out(1)]
    outs += [(flat_shape, flat_spec), (flat_shape, flat_spec)]
    n_slots = (N_HEAD_PAIRS // 2) * (tm // SUB_TILE)
    return pl.pallas_call(
        _inproj_kernel,
        grid=(b, s // tm),
        in_specs=[
            pl.BlockSpec((1, tm, D_MODEL), lambda i, t: (i, t, 0)),
            pl.BlockSpec((1, D_MODEL), lambda i, t: (0, 0)),
            pl.BlockSpec((D_MODEL, D_IN), lambda i, t: (0, 0), pipeline_mode=pl.Buffered(1)),
            pl.BlockSpec((tm, LANES), lambda i, t: (t, 0)),
            pl.BlockSpec((tm, LANES), lambda i, t: (t, 0)),
        ],
        out_specs=[spec for _, spec in outs],
        out_shape=[shape for shape, _ in outs],
        scratch_shapes=[pltpu.VMEM((n_slots, SUB_TILE, LANES), jnp.uint32),
                        pltpu.VMEM((n_slots, SUB_TILE, LANES), jnp.uint32)],
        compiler_params=pltpu.CompilerParams(
            dimension_semantics=("arbitrary", "arbitrary"), vmem_limit_bytes=VMEM_LIMIT),
        name="inproj",
    )(x, g, w, cos, sin)


def _merge(a, b):
    acc_a, m_a, l_a = a
    acc_b, m_b, l_b = b
    m = jnp.maximum(m_a, m_b)
    w_a = jnp.exp2(m_a - m)
    w_b = jnp.exp2(m_b - m)
    return w_a * acc_a + w_b * acc_b, m, w_a * l_a + w_b * l_b


def _attn_kernel(*refs, n_chunks):
    halo = n_chunks > 1
    it = iter(refs)
    q_refs = [next(it) for _ in VIEWS]
    k_refs = [[next(it) for _ in range(3 if halo else 1)] for _ in VIEWS]
    v_refs = [[next(it) for _ in range(3 if halo else 1)] for _ in VIEWS]
    ga_ref = next(it)
    bias_ref = next(it)
    if not halo:
        filler_ref = next(it)
        k_refs = [[parts[0], filler_ref, filler_ref] for parts in k_refs]
        v_refs = [[parts[0], filler_ref, filler_ref] for parts in v_refs]
    o_ref = next(it)
    scratch = tuple(it)

    ci = pl.program_id(1)
    first_chunk = (ci == 0).astype(jnp.int32)
    last_chunk = (ci == n_chunks - 1).astype(jnp.int32)

    word_lane = lax.broadcasted_iota(jnp.int32, (Q_BLOCK // 2, LANES), 1)
    all_bits = jnp.full((Q_BLOCK // 2, LANES), 0xFFFFFFFF, jnp.uint32)
    no_bits = jnp.zeros((Q_BLOCK // 2, LANES), jnp.uint32)
    in_head0 = (word_lane % (LANES // 2)) < HEAD_DIM // 2
    keep_head0 = jnp.where(in_head0, all_bits, no_bits)
    keep_head1 = jnp.where(in_head0, no_bits, all_bits)
    out_head0 = lax.broadcasted_iota(jnp.int32, (Q_BLOCK, LANES), 1) < HEAD_DIM
    ones = jnp.ones((K_BLOCK, LANES), BF16)

    def window(parts, hp, residue, lo, hi):
        lanes = slice(residue * LANES, (residue + 1) * LANES)
        rows = parts[0].shape[2]
        halo_hp = hp if halo else 0
        pieces = []
        if lo < 0:
            pieces.append(parts[1][0, halo_hp, 0, HALF + lo:HALF, lanes])
        pieces.append(parts[0][0, hp, max(lo, 0):min(hi, rows), lanes])
        if hi > rows:
            pieces.append(parts[2][0, halo_hp, 0, 0:hi - rows, lanes])
        return pieces

    def rows_of(pieces):
        return pieces[0] if len(pieces) == 1 else jnp.concatenate(pieces, axis=0)

    def block(q, kwin, vwin, bias_index):
        q = pltpu.bitcast(q, jnp.uint32)
        bias = bias_ref[bias_index]
        q2 = jnp.concatenate([pltpu.bitcast(q & keep_head0, BF16), pltpu.bitcast(q & keep_head1, BF16)], axis=0)
        s = lax.dot_general(q2, kwin, (((1,), (1,)), ((), ())), preferred_element_type=F32)
        ms, ps = [], []
        for h in range(2):
            sh = s[h * Q_BLOCK:(h + 1) * Q_BLOCK] + bias
            m = jnp.max(sh, axis=-1, keepdims=True)
            ps.append(jnp.exp2(sh - m).astype(BF16))
            ms.append(jnp.broadcast_to(m, (Q_BLOCK, LANES)))
        pv = jnp.dot(jnp.concatenate(ps, axis=0), jnp.concatenate([vwin, ones], axis=1),
                     preferred_element_type=F32)
        head0, head1 = pv[:Q_BLOCK], pv[Q_BLOCK:]
        return (jnp.where(out_head0, head0[:, :LANES], head1[:, :LANES]),
                jnp.where(out_head0, ms[0], ms[1]),
                jnp.where(out_head0, head0[:, LANES:], head1[:, LANES:]))

    def bias_index(base, is_first_block, is_last_block):
        first = first_chunk if is_first_block else 0
        last = last_chunk if is_last_block else 0
        return base + 2 * first + last

    q4_ref, q16_ref = q_refs
    (k4, k16), (v4, v16) = k_refs, v_refs
    rows16 = CHUNK // 16
    rows4 = CHUNK // 4
    for hp in range(PAIRS_PER_STEP):
        scratch_set = hp % SCRATCH_SETS
        acc_ref, m_ref, l_ref, tok_ref = scratch[4 * scratch_set:4 * scratch_set + 4]

        for r in range(16):
            a, b = divmod(r, 4)
            state = block(q16_ref[0, hp, :, r * LANES:(r + 1) * LANES],
                          rows_of(window(k16, hp, r, -HALF, rows16 + HALF)),
                          rows_of(window(v16, hp, r, -HALF, rows16 + HALF)),
                          bias_index(0, True, True))
            for ref, value in zip((acc_ref, m_ref, l_ref), state):
                ref[b, pl.ds(a, Q_BLOCK, stride=4), :] = value

        for b in range(4):
            for start in range(0, rows4, Q_BLOCK):
                rows = slice(start, start + Q_BLOCK)
                state = block(q4_ref[0, hp, rows, b * LANES:(b + 1) * LANES],
                              rows_of(window(k4, hp, b, start - HALF, start + Q_BLOCK + HALF)),
                              rows_of(window(v4, hp, b, start - HALF, start + Q_BLOCK + HALF)),
                              bias_index(0, start == 0, start == rows4 - Q_BLOCK))
                merged = _merge(state, (acc_ref[b, rows, :], m_ref[b, rows, :], l_ref[b, rows, :]))
                for ref, value in zip((acc_ref, m_ref, l_ref), merged):
                    ref[b, rows, :] = value

        for w0 in range(0, rows4, GROUP_Q_ROWS):
            lo, hi = w0 - GROUP_PAD, w0 - GROUP_PAD + GROUP_K_ROWS
            rows = slice(w0, w0 + GROUP_Q_ROWS)
            q = jnp.concatenate([q4_ref[0, hp, rows, b * LANES:(b + 1) * LANES] for b in range(4)], axis=0)
            kwin = jnp.concatenate([piece for b in range(4) for piece in window(k4, hp, b, lo, hi)], axis=0)
            vwin = jnp.concatenate([piece for b in range(4) for piece in window(v4, hp, b, lo, hi)], axis=0)
            state = block(q, kwin, vwin, bias_index(GROUPED_BIAS, w0 == 0, w0 == rows4 - GROUP_Q_ROWS))
            prev = tuple(jnp.concatenate([ref[b, rows, :] for b in range(4)], axis=0)
                         for ref in (acc_ref, m_ref, l_ref))
            acc, _, l = _merge(state, prev)
            out = acc / l
            tokens = slice(4 * w0, 4 * w0 + Q_BLOCK)
            for b in range(4):
                tok_ref[pl.ds(4 * w0 + b, GROUP_Q_ROWS, stride=4), :] = out[b * GROUP_Q_ROWS:(b + 1) * GROUP_Q_ROWS]
            o_ref[0, hp, tokens, :] = (tok_ref[tokens, :] * ga_ref[0, hp, tokens, :].astype(F32)).astype(BF16)


def _attn(q_views, k_views, v_views, ga, bias):
    b, _, rows, _ = q_views[0].shape
    s = rows * VIEWS[0]
    n_chunks = s // CHUNK
    halo = n_chunks > 1
    args, specs = [], []

    pairs = PAIRS_PER_STEP

    def center(view, d):
        args.append(view)
        specs.append(pl.BlockSpec((1, pairs, CHUNK // d, LANES * d), lambda i, c, p: (i, p, c, 0)))

    def halos(arr, d):
        blocks_per_chunk = CHUNK // d // HALF
        n_blocks = s // d // HALF
        view = arr.reshape(b, N_HEAD_PAIRS, n_blocks, HALF, LANES * d)
        shape = (1, pairs, 1, HALF, LANES * d)
        args.append(view)
        specs.append(pl.BlockSpec(
            shape, lambda i, c, p: (i, p, jnp.maximum(c * blocks_per_chunk - 1, 0), 0, 0)))
        args.append(view)
        specs.append(pl.BlockSpec(
            shape, lambda i, c, p: (i, p, jnp.minimum((c + 1) * blocks_per_chunk, n_blocks - 1), 0, 0)))

    for view, d in zip(q_views, VIEWS):
        center(view, d)
    for views in (k_views, v_views):
        for view, d in zip(views, VIEWS):
            center(view, d)
            if halo:
                halos(view, d)
    args.append(ga)
    specs.append(pl.BlockSpec((1, pairs, CHUNK, LANES), lambda i, c, p: (i, p, c, 0)))
    args.append(bias)
    specs.append(pl.BlockSpec(bias.shape, lambda i, c, p: (0, 0, 0)))
    if not halo:
        filler_shape = (1, 1, 1, HALF, LANES * VIEWS[-1])
        args.append(jnp.zeros(filler_shape, BF16))
        specs.append(pl.BlockSpec(filler_shape, lambda i, c, p: (0, 0, 0, 0, 0)))

    state = pltpu.VMEM((4, CHUNK // 4, LANES), F32)
    token_order = pltpu.VMEM((CHUNK, LANES), F32)
    return pl.pallas_call(
        functools.partial(_attn_kernel, n_chunks=n_chunks),
        grid=(b, n_chunks, N_HEAD_PAIRS // pairs),
        in_specs=specs,
        out_specs=pl.BlockSpec((1, pairs, CHUNK, LANES), lambda i, c, p: (i, p, c, 0)),
        out_shape=jax.ShapeDtypeStruct((b, N_HEAD_PAIRS, s, LANES), BF16),
        scratch_shapes=[state, state, state, token_order] * SCRATCH_SETS,
        compiler_params=pltpu.CompilerParams(
            dimension_semantics=("arbitrary", "arbitrary", "arbitrary"), vmem_limit_bytes=VMEM_LIMIT),
        name="attn",
    )(*args)


def _pool_bands():
    i = np.arange(POOL_BLOCK)[:, None]
    j = np.arange(POOL_K)[None, :]
    rel = j - POOL_HALO - i
    return np.stack([(rel >= -(w // 2)) & (rel <= w // 2 - 1) for w in POOL_WINDOWS]).astype(np.float32)


def _outproj_kernel(ya_ref, u_ref, up_ref, un_ref, gp_ref, x_ref, band_ref, wp_ref, ps_ref, wo_ref, fg_ref,
                    o_ref, *, seq_len, final):
    tm = OUT_TILE
    t = pl.program_id(1)
    n_t = pl.num_programs(1)
    halo_zero = jnp.zeros((POOL_HALO, D_POOL), BF16)
    ub = jnp.concatenate(
        [jnp.where(t > 0, up_ref[0, 0], halo_zero), u_ref[0], jnp.where(t < n_t - 1, un_ref[0, 0], halo_zero),
         jnp.zeros((POOL_K - POOL_BLOCK - 2 * POOL_HALO, D_POOL), BF16)], axis=0)

    pos = t * tm + lax.broadcasted_iota(jnp.int32, (tm, POOL_GROUP_DIM), 0)
    parts = [ya_ref[0, p] for p in range(N_HEAD_PAIRS)]
    for g, w in enumerate(POOL_WINDOWS):
        lanes = slice(g * POOL_GROUP_DIM, (g + 1) * POOL_GROUP_DIM)
        total = jnp.concatenate(
            [jnp.dot(band_ref[g], ub[r0:r0 + POOL_K, lanes], preferred_element_type=F32)
             for r0 in range(0, tm, POOL_BLOCK)], axis=0)
        count = (jnp.minimum(pos + w // 2, seq_len) - jnp.maximum(pos - w // 2, 0)).astype(F32)
        pooled = total / count - u_ref[0, :, lanes].astype(F32)
        mixed = jnp.dot(pooled.astype(BF16), wp_ref[g], preferred_element_type=F32)
        parts.append((mixed * ps_ref[:, lanes] * gp_ref[0, :, lanes].astype(F32)).astype(BF16))
    y = jnp.concatenate(parts, axis=1)
    out = x_ref[0] + jnp.dot(y, wo_ref[...], preferred_element_type=F32)
    if final:
        inv = lax.rsqrt(jnp.mean(out * out, axis=-1, keepdims=True) + RMS_EPS)
        out = out * inv * fg_ref[...]
    o_ref[0] = out


def _outproj(ya, u, gp, x, bands, wp, ps, wo, fg, final):
    b, s, _ = x.shape
    tm = OUT_TILE
    halo_blocks = s // POOL_HALO
    per_tile = tm // POOL_HALO
    u_halo = u.reshape(b, halo_blocks, POOL_HALO, D_POOL)
    const2 = lambda i, t: (0, 0)
    const3 = lambda i, t: (0, 0, 0)
    return pl.pallas_call(
        functools.partial(_outproj_kernel, seq_len=s, final=final),
        grid=(b, s // tm),
        in_specs=[
            pl.BlockSpec((1, N_HEAD_PAIRS, tm, LANES), lambda i, t: (i, 0, t, 0)),
            pl.BlockSpec((1, tm, D_POOL), lambda i, t: (i, t, 0)),
            pl.BlockSpec((1, 1, POOL_HALO, D_POOL),
                         lambda i, t: (i, jnp.maximum(t * per_tile - 1, 0), 0, 0)),
            pl.BlockSpec((1, 1, POOL_HALO, D_POOL),
                         lambda i, t: (i, jnp.minimum((t + 1) * per_tile, halo_blocks - 1), 0, 0)),
            pl.BlockSpec((1, tm, D_POOL), lambda i, t: (i, t, 0)),
            pl.BlockSpec((1, tm, D_MODEL), lambda i, t: (i, t, 0)),
            pl.BlockSpec(bands.shape, const3),
            pl.BlockSpec(wp.shape, const3),
            pl.BlockSpec((1, D_POOL), const2),
            pl.BlockSpec((D_MIX, D_MODEL), const2),
            pl.BlockSpec((1, D_MODEL), const2),
        ],
        out_specs=pl.BlockSpec((1, tm, D_MODEL), lambda i, t: (i, t, 0)),
        out_shape=jax.ShapeDtypeStruct((b, s, D_MODEL), F32),
        compiler_params=pltpu.CompilerParams(
            dimension_semantics=("arbitrary", "arbitrary"), vmem_limit_bytes=VMEM_LIMIT),
        name="outproj",
    )(ya, u, u_halo, u_halo, gp, x, bands, wp, ps, wo, fg)


def _rope_tables(seq_len):
    inv_freq = ROPE_THETA ** (-jnp.arange(0, HEAD_DIM, 2, dtype=F32) / HEAD_DIM)
    ang = jnp.arange(seq_len, dtype=F32)[:, None] * inv_freq[None, :]
    cos, sin = jnp.cos(ang), jnp.sin(ang)
    return (jnp.concatenate([cos, cos, cos, cos], axis=1),
            jnp.concatenate([-sin, -sin, sin, sin], axis=1))


def _trunk(x, norm_g, w_in, w_pool, pool_scale, w_out, final_norm_g, bias, bands):
    depth = norm_g.shape[0]
    cos, sin = _rope_tables(x.shape[1])
    fg = final_norm_g.reshape(1, D_MODEL)
    for i in range(depth):
        *qkv, ga, u, gp = _inproj(x, norm_g[i].reshape(1, D_MODEL), w_in[i], cos, sin)
        ya = _attn(qkv[0:2], qkv[2:4], qkv[4:6], ga, bias)
        x = _outproj(ya, u, gp, x, bands, w_pool[i], pool_scale[i].reshape(1, D_POOL), w_out[i], fg,
                     final=(i == depth - 1))
    return x


def kernel(x_prompt, x_sample, norm_g, w_in, w_pool, pool_scale, w_out, final_norm_g):
    order = _rope_column_order()[:D_ATTN]
    w_in = jnp.concatenate(
        [w_in[:, :, :D_ATTN][:, :, order], w_in[:, :, D_ATTN:2 * D_ATTN][:, :, order],
         w_in[:, :, 2 * D_ATTN:]], axis=2).astype(BF16)
    w_pool = w_pool.astype(BF16)
    w_out = w_out.astype(BF16)
    bias = jnp.asarray(_band_bias())
    bands = jnp.asarray(_pool_bands(), dtype=BF16)
    run = functools.partial(_trunk, norm_g=norm_g, w_in=w_in, w_pool=w_pool, pool_scale=pool_scale,
                            w_out=w_out, final_norm_g=final_norm_g, bias=bias, bands=bands)
    return (run(x_prompt), run(x_sample))
```

```python
import functools

import numpy as np
import jax
import jax.numpy as jnp
from jax import lax
from jax.experimental import pallas as pl
from jax.experimental.pallas import tpu as pltpu

D_MODEL = 1024
D_ATTN = 1024
D_POOL = 1024
D_MIX = D_ATTN + D_POOL
D_IN = 4 * D_ATTN + 2 * D_POOL
HEAD_DIM = 64
N_HEADS = D_ATTN // HEAD_DIM
LANES = 128
N_HEAD_PAIRS = D_ATTN // LANES
DILATIONS = (1, 4, 16)
VIEWS = (4, 16)
HALF = 64
POOL_WINDOWS = (2, 4, 8, 16)
POOL_GROUP_DIM = D_POOL // len(POOL_WINDOWS)
POOL_HALO = 16
POOL_BLOCK = 128
POOL_K = 256
ROPE_THETA = 10000.0
RMS_EPS = 1e-6

Q_BLOCK = 2 * HALF
K_BLOCK = Q_BLOCK + 2 * HALF
CHUNK = Q_BLOCK * DILATIONS[-1]
PAIRS_PER_STEP = 4
SCRATCH_SETS = 2
TOKEN_TILE = 512
OUT_TILE = 1024
SUB_TILE = 256
MASK_VALUE = -1e30
QUERY_SCALE = HEAD_DIM ** -0.5 * float(np.log2(np.e))
VMEM_LIMIT = 56 * 1024 * 1024

F32 = jnp.float32
BF16 = jnp.bfloat16


def _rope_column_order():
    order = []
    half = HEAD_DIM // 2
    for pair in range(N_HEAD_PAIRS):
        h0, h1 = 2 * pair * HEAD_DIM, (2 * pair + 1) * HEAD_DIM
        order += list(range(h0, h0 + half)) + list(range(h1, h1 + half))
        order += list(range(h0 + half, h0 + HEAD_DIM)) + list(range(h1 + half, h1 + HEAD_DIM))
    order = np.asarray(order, dtype=np.int32)
    return np.concatenate([order, D_ATTN + order, np.arange(2 * D_ATTN, D_IN, dtype=np.int32)])


GROUPED_BIAS = 4
GROUP_Q_ROWS = Q_BLOCK // 4
GROUP_K_ROWS = K_BLOCK // 4
GROUP_PAD = HALF // 4


def _band_bias():
    qi = np.arange(Q_BLOCK)[:, None]
    kj = np.arange(K_BLOCK)[None, :]
    out = []

    def variants(band, first_ok, last_ok):
        for first in (False, True):
            for last in (False, True):
                valid = band & (first_ok if first else True) & (last_ok if last else True)
                out.append(np.where(valid, 0.0, MASK_VALUE))

    variants(np.abs(kj - HALF - qi) <= HALF, kj >= HALF, kj < HALF + Q_BLOCK)
    qb, qw = qi // GROUP_Q_ROWS, qi % GROUP_Q_ROWS
    kb, kw = kj // GROUP_K_ROWS, kj % GROUP_K_ROWS
    rel = 4 * (kw - GROUP_PAD - qw) + (kb - qb)
    variants(np.abs(rel) <= HALF, kw >= GROUP_PAD, kw < GROUP_PAD + GROUP_Q_ROWS)
    return np.stack(out).astype(np.float32)


def _silu(g):
    return g / (1.0 + jnp.exp(-g))


def _inproj_kernel(x_ref, g_ref, w_ref, cos_ref, sin_ref,
                   q4_ref, q16_ref, k4_ref, k16_ref, v4_ref, v16_ref,
                   ga_ref, u_ref, gp_ref, nat_ref, by4_ref):
    for sub in range(TOKEN_TILE // SUB_TILE):
        _inproj_rows(sub, x_ref, g_ref, w_ref, cos_ref, sin_ref,
                     (q4_ref, q16_ref), (k4_ref, k16_ref), (v4_ref, v16_ref),
                     ga_ref, u_ref, gp_ref, nat_ref, by4_ref)


def _inproj_rows(sub, x_ref, g_ref, w_ref, cos_ref, sin_ref, q_refs, k_refs, v_refs,
                 ga_ref, u_ref, gp_ref, nat_ref, by4_ref):
    n = SUB_TILE
    rows = slice(sub * n, (sub + 1) * n)
    x = x_ref[0, rows]
    inv = lax.rsqrt(jnp.mean(x * x, axis=-1, keepdims=True) + RMS_EPS)
    h = (x * inv * g_ref[...]).astype(BF16)
    cos = cos_ref[rows]
    sin = sin_ref[rows]

    def proj(section):
        return jnp.dot(h, w_ref[:, section * D_ATTN:(section + 1) * D_ATTN],
                       preferred_element_type=F32)

    def rope(t):
        return t * cos + pltpu.roll(t, LANES // 2, axis=1) * sin

    def store_views(p, tiles, outs):
        out4, out16 = outs
        slot = sub * (N_HEAD_PAIRS // 2) + p // 2
        nat_ref[slot] = pltpu.pack_elementwise(tiles, packed_dtype=BF16)

        def halves(words):
            return [pltpu.unpack_elementwise(words, index=i, packed_dtype=BF16, unpacked_dtype=F32).astype(BF16)
                    for i in range(2)]

        for b in range(4):
            part = nat_ref[slot, pl.ds(b, n // 4, stride=4), :]
            by4_ref[slot, b * (n // 4):(b + 1) * (n // 4), :] = part
            for i, half in enumerate(halves(part)):
                out4[0, p + i, sub * (n // 4):(sub + 1) * (n // 4), b * LANES:(b + 1) * LANES] = half
        for a in range(4):
            stack = by4_ref[slot, pl.ds(a, n // 4, stride=4), :]
            for i, half in enumerate(halves(stack)):
                for b in range(4):
                    r = 4 * a + b
                    out16[0, p + i, sub * (n // 16):(sub + 1) * (n // 16), r * LANES:(r + 1) * LANES] = (
                        half[b * (n // 16):(b + 1) * (n // 16)])

    def pair_tiles(acc, p, finish):
        return [finish(acc[:, q * LANES:(q + 1) * LANES]) for q in (p, p + 1)]

    acc = proj(0)
    for p in range(0, N_HEAD_PAIRS, 2):
        store_views(p, pair_tiles(acc, p, lambda t: rope(t) * QUERY_SCALE), q_refs)
    acc = proj(1)
    for p in range(0, N_HEAD_PAIRS, 2):
        store_views(p, pair_tiles(acc, p, rope), k_refs)
    acc = proj(2)
    for p in range(0, N_HEAD_PAIRS, 2):
        store_views(p, pair_tiles(acc, p, lambda t: t), v_refs)
    acc = proj(3)
    for p in range(N_HEAD_PAIRS):
        ga_ref[0, p, rows] = _silu(acc[:, p * LANES:(p + 1) * LANES]).astype(BF16)
    u_ref[0, rows] = proj(4).astype(BF16)
    gp_ref[0, rows] = _silu(proj(5)).astype(BF16)


def _inproj(x, g, w, cos, sin):
    b, s, _ = x.shape
    tm = TOKEN_TILE

    def pair_out(d):
        shape = jax.ShapeDtypeStruct((b, N_HEAD_PAIRS, s // d, LANES * d), BF16)
        spec = pl.BlockSpec((1, N_HEAD_PAIRS, tm // d, LANES * d), lambda i, t: (i, 0, t, 0))
        return shape, spec

    flat_shape = jax.ShapeDtypeStruct((b, s, D_POOL), BF16)
    flat_spec = pl.BlockSpec((1, tm, D_POOL), lambda i, t: (i, t, 0))
    outs = [pair_out(d) for _ in range(3) for d in VIEWS] + [pair_out(1)]
    outs += [(flat_shape, flat_spec), (flat_shape, flat_spec)]
    n_slots = (N_HEAD_PAIRS // 2) * (tm // SUB_TILE)
    return pl.pallas_call(
        _inproj_kernel,
        grid=(b, s // tm),
        in_specs=[
            pl.BlockSpec((1, tm, D_MODEL), lambda i, t: (i, t, 0)),
            pl.BlockSpec((1, D_MODEL), lambda i, t: (0, 0)),
            pl.BlockSpec((D_MODEL, D_IN), lambda i, t: (0, 0), pipeline_mode=pl.Buffered(1)),
            pl.BlockSpec((tm, LANES), lambda i, t: (t, 0)),
            pl.BlockSpec((tm, LANES), lambda i, t: (t, 0)),
        ],
        out_specs=[spec for _, spec in outs],
        out_shape=[shape for shape, _ in outs],
        scratch_shapes=[pltpu.VMEM((n_slots, SUB_TILE, LANES), jnp.int32),
                        pltpu.VMEM((n_slots, SUB_TILE, LANES), jnp.int32)],
        compiler_params=pltpu.CompilerParams(
            dimension_semantics=("arbitrary", "arbitrary"), vmem_limit_bytes=VMEM_LIMIT),
        name="inproj",
    )(x, g, w, cos, sin)


def _merge(a, b):
    acc_a, m_a, l_a = a
    acc_b, m_b, l_b = b
    m = jnp.maximum(m_a, m_b)
    w_a = jnp.exp2(m_a - m)
    w_b = jnp.exp2(m_b - m)
    return w_a * acc_a + w_b * acc_b, m, w_a * l_a + w_b * l_b


def _attn_kernel(*refs, n_chunks):
    halo = n_chunks > 1
    it = iter(refs)
    q_refs = [next(it) for _ in VIEWS]
    k_refs = [[next(it) for _ in range(3 if halo else 1)] for _ in VIEWS]
    v_refs = [[next(it) for _ in range(3 if halo else 1)] for _ in VIEWS]
    ga_ref = next(it)
    bias_ref = next(it)
    if not halo:
        filler_ref = next(it)
        k_refs = [[parts[0], filler_ref, filler_ref] for parts in k_refs]
        v_refs = [[parts[0], filler_ref, filler_ref] for parts in v_refs]
    o_ref = next(it)
    scratch = tuple(it)

    ci = pl.program_id(1)
    first_chunk = (ci == 0).astype(jnp.int32)
    last_chunk = (ci == n_chunks - 1).astype(jnp.int32)

    word_lane = lax.broadcasted_iota(jnp.int32, (Q_BLOCK // 2, LANES), 1)
    all_bits = jnp.full((Q_BLOCK // 2, LANES), 0xFFFFFFFF, jnp.uint32)
    no_bits = jnp.zeros((Q_BLOCK // 2, LANES), jnp.uint32)
    in_head0 = (word_lane % (LANES // 2)) < HEAD_DIM // 2
    keep_head0 = jnp.where(in_head0, all_bits, no_bits)
    keep_head1 = jnp.where(in_head0, no_bits, all_bits)
    out_head0 = lax.broadcasted_iota(jnp.int32, (Q_BLOCK, LANES), 1) < HEAD_DIM
    ones = jnp.ones((K_BLOCK, LANES), BF16)

    def window(parts, hp, residue, lo, hi):
        lanes = slice(residue * LANES, (residue + 1) * LANES)
        rows = parts[0].shape[2]
        halo_hp = hp if halo else 0
        pieces = []
        if lo < 0:
            pieces.append(parts[1][0, halo_hp, 0, HALF + lo:HALF, lanes])
        pieces.append(parts[0][0, hp, max(lo, 0):min(hi, rows), lanes])
        if hi > rows:
            pieces.append(parts[2][0, halo_hp, 0, 0:hi - rows, lanes])
        return pieces

    def rows_of(pieces):
        return pieces[0] if len(pieces) == 1 else jnp.concatenate(pieces, axis=0)

    def block(q, kwin, vwin, bias_index):
        q = pltpu.bitcast(q, jnp.uint32)
        bias = bias_ref[bias_index]
        q2 = jnp.concatenate([pltpu.bitcast(q & keep_head0, BF16), pltpu.bitcast(q & keep_head1, BF16)], axis=0)
        s = lax.dot_general(q2, kwin, (((1,), (1,)), ((), ())), preferred_element_type=F32)
        ms, ps = [], []
        for h in range(2):
            sh = s[h * Q_BLOCK:(h + 1) * Q_BLOCK] + bias
            m = jnp.max(sh, axis=-1, keepdims=True)
            ps.append(jnp.exp2(sh - m).astype(BF16))
            ms.append(jnp.broadcast_to(m, (Q_BLOCK, LANES)))
        pv = jnp.dot(jnp.concatenate(ps, axis=0), jnp.concatenate([vwin, ones], axis=1),
                     preferred_element_type=F32)
        head0, head1 = pv[:Q_BLOCK], pv[Q_BLOCK:]
        return (jnp.where(out_head0, head0[:, :LANES], head1[:, :LANES]),
                jnp.where(out_head0, ms[0], ms[1]),
                jnp.where(out_head0, head0[:, LANES:], head1[:, LANES:]))

    def bias_index(base, is_first_block, is_last_block):
        first = first_chunk if is_first_block else 0
        last = last_chunk if is_last_block else 0
        return base + 2 * first + last

    q4_ref, q16_ref = q_refs
    (k4, k16), (v4, v16) = k_refs, v_refs
    rows16 = CHUNK // 16
    rows4 = CHUNK // 4
    for hp in range(PAIRS_PER_STEP):
        scratch_set = hp % SCRATCH_SETS
        acc_ref, m_ref, l_ref, tok_ref = scratch[4 * scratch_set:4 * scratch_set + 4]

        for r in range(16):
            a, b = divmod(r, 4)
            state = block(q16_ref[0, hp, :, r * LANES:(r + 1) * LANES],
                          rows_of(window(k16, hp, r, -HALF, rows16 + HALF)),
                          rows_of(window(v16, hp, r, -HALF, rows16 + HALF)),
                          bias_index(0, True, True))
            for ref, value in zip((acc_ref, m_ref, l_ref), state):
                ref[b, pl.ds(a, Q_BLOCK, stride=4), :] = value

        for b in range(4):
            for start in range(0, rows4, Q_BLOCK):
                rows = slice(start, start + Q_BLOCK)
                state = block(q4_ref[0, hp, rows, b * LANES:(b + 1) * LANES],
                              rows_of(window(k4, hp, b, start - HALF, start + Q_BLOCK + HALF)),
                              rows_of(window(v4, hp, b, start - HALF, start + Q_BLOCK + HALF)),
                              bias_index(0, start == 0, start == rows4 - Q_BLOCK))
                merged = _merge(state, (acc_ref[b, rows, :], m_ref[b, rows, :], l_ref[b, rows, :]))
                for ref, value in zip((acc_ref, m_ref, l_ref), merged):
                    ref[b, rows, :] = value

        for w0 in range(0, rows4, GROUP_Q_ROWS):
            lo, hi = w0 - GROUP_PAD, w0 - GROUP_PAD + GROUP_K_ROWS
            rows = slice(w0, w0 + GROUP_Q_ROWS)
            q = jnp.concatenate([q4_ref[0, hp, rows, b * LANES:(b + 1) * LANES] for b in range(4)], axis=0)
            kwin = jnp.concatenate([piece for b in range(4) for piece in window(k4, hp, b, lo, hi)], axis=0)
            vwin = jnp.concatenate([piece for b in range(4) for piece in window(v4, hp, b, lo, hi)], axis=0)
            state = block(q, kwin, vwin, bias_index(GROUPED_BIAS, w0 == 0, w0 == rows4 - GROUP_Q_ROWS))
            prev = tuple(jnp.concatenate([ref[b, rows, :] for b in range(4)], axis=0)
                         for ref in (acc_ref, m_ref, l_ref))
            acc, _, l = _merge(state, prev)
            out = acc / l
            tokens = slice(4 * w0, 4 * w0 + Q_BLOCK)
            for b in range(4):
                tok_ref[pl.ds(4 * w0 + b, GROUP_Q_ROWS, stride=4), :] = out[b * GROUP_Q_ROWS:(b + 1) * GROUP_Q_ROWS]
            o_ref[0, hp, tokens, :] = (tok_ref[tokens, :] * ga_ref[0, hp, tokens, :].astype(F32)).astype(BF16)


def _attn(q_views, k_views, v_views, ga, bias):
    b, _, rows, _ = q_views[0].shape
    s = rows * VIEWS[0]
    n_chunks = s // CHUNK
    halo = n_chunks > 1
    args, specs = [], []

    pairs = PAIRS_PER_STEP

    def center(view, d):
        args.append(view)
        specs.append(pl.BlockSpec((1, pairs, CHUNK // d, LANES * d), lambda i, c, p: (i, p, c, 0)))

    def halos(arr, d):
        blocks_per_chunk = CHUNK // d // HALF
        n_blocks = s // d // HALF
        view = arr.reshape(b, N_HEAD_PAIRS, n_blocks, HALF, LANES * d)
        shape = (1, pairs, 1, HALF, LANES * d)
        args.append(view)
        specs.append(pl.BlockSpec(
            shape, lambda i, c, p: (i, p, jnp.maximum(c * blocks_per_chunk - 1, 0), 0, 0)))
        args.append(view)
        specs.append(pl.BlockSpec(
            shape, lambda i, c, p: (i, p, jnp.minimum((c + 1) * blocks_per_chunk, n_blocks - 1), 0, 0)))

    for view, d in zip(q_views, VIEWS):
        center(view, d)
    for views in (k_views, v_views):
        for view, d in zip(views, VIEWS):
            center(view, d)
            if halo:
                halos(view, d)
    args.append(ga)
    specs.append(pl.BlockSpec((1, pairs, CHUNK, LANES), lambda i, c, p: (i, p, c, 0)))
    args.append(bias)
    specs.append(pl.BlockSpec(bias.shape, lambda i, c, p: (0, 0, 0)))
    if not halo:
        filler_shape = (1, 1, 1, HALF, LANES * VIEWS[-1])
        args.append(jnp.zeros(filler_shape, BF16))
        specs.append(pl.BlockSpec(filler_shape, lambda i, c, p: (0, 0, 0, 0, 0)))

    state = pltpu.VMEM((4, CHUNK // 4, LANES), F32)
    token_order = pltpu.VMEM((CHUNK, LANES), F32)
    return pl.pallas_call(
        functools.partial(_attn_kernel, n_chunks=n_chunks),
        grid=(b, n_chunks, N_HEAD_PAIRS // pairs),
        in_specs=specs,
        out_specs=pl.BlockSpec((1, pairs, CHUNK, LANES), lambda i, c, p: (i, p, c, 0)),
        out_shape=jax.ShapeDtypeStruct((b, N_HEAD_PAIRS, s, LANES), BF16),
        scratch_shapes=[state, state, state, token_order] * SCRATCH_SETS,
        compiler_params=pltpu.CompilerParams(
            dimension_semantics=("arbitrary", "arbitrary", "arbitrary"), vmem_limit_bytes=VMEM_LIMIT),
        name="attn",
    )(*args)


def _pool_bands():
    i = np.arange(POOL_BLOCK)[:, None]
    j = np.arange(POOL_K)[None, :]
    rel = j - POOL_HALO - i
    return np.stack([(rel >= -(w // 2)) & (rel <= w // 2 - 1) for w in POOL_WINDOWS]).astype(np.float32)


def _outproj_kernel(ya_ref, u_ref, up_ref, un_ref, gp_ref, x_ref, band_ref, wp_ref, ps_ref, wo_ref, fg_ref,
                    o_ref, *, seq_len, final):
    tm = OUT_TILE
    t = pl.program_id(1)
    n_t = pl.num_programs(1)
    halo_zero = jnp.zeros((POOL_HALO, D_POOL), BF16)
    ub = jnp.concatenate(
        [jnp.where(t > 0, up_ref[0, 0], halo_zero), u_ref[0], jnp.where(t < n_t - 1, un_ref[0, 0], halo_zero),
         jnp.zeros((POOL_K - POOL_BLOCK - 2 * POOL_HALO, D_POOL), BF16)], axis=0)

    pos = t * tm + lax.broadcasted_iota(jnp.int32, (tm, POOL_GROUP_DIM), 0)
    parts = [ya_ref[0, p] for p in range(N_HEAD_PAIRS)]
    for g, w in enumerate(POOL_WINDOWS):
        lanes = slice(g * POOL_GROUP_DIM, (g + 1) * POOL_GROUP_DIM)
        total = jnp.concatenate(
            [jnp.dot(band_ref[g], ub[r0:r0 + POOL_K, lanes], preferred_element_type=F32)
             for r0 in range(0, tm, POOL_BLOCK)], axis=0)
        count = (jnp.minimum(pos + w // 2, seq_len) - jnp.maximum(pos - w // 2, 0)).astype(F32)
        pooled = total / count - u_ref[0, :, lanes].astype(F32)
        mixed = jnp.dot(pooled.astype(BF16), wp_ref[g], preferred_element_type=F32)
        parts.append((mixed * ps_ref[:, lanes] * gp_ref[0, :, lanes].astype(F32)).astype(BF16))
    y = jnp.concatenate(parts, axis=1)
    out = x_ref[0] + jnp.dot(y, wo_ref[...], preferred_element_type=F32)
    if final:
        inv = lax.rsqrt(jnp.mean(out * out, axis=-1, keepdims=True) + RMS_EPS)
        out = out * inv * fg_ref[...]
    o_ref[0] = out


def _outproj(ya, u, gp, x, bands, wp, ps, wo, fg, final):
    b, s, _ = x.shape
    tm = OUT_TILE
    halo_blocks = s // POOL_HALO
    per_tile = tm // POOL_HALO
    u_halo = u.reshape(b, halo_blocks, POOL_HALO, D_POOL)
    const2 = lambda i, t: (0, 0)
    const3 = lambda i, t: (0, 0, 0)
    return pl.pallas_call(
        functools.partial(_outproj_kernel, seq_len=s, final=final),
        grid=(b, s // tm),
        in_specs=[
            pl.BlockSpec((1, N_HEAD_PAIRS, tm, LANES), lambda i, t: (i, 0, t, 0)),
            pl.BlockSpec((1, tm, D_POOL), lambda i, t: (i, t, 0)),
            pl.BlockSpec((1, 1, POOL_HALO, D_POOL),
                         lambda i, t: (i, jnp.maximum(t * per_tile - 1, 0), 0, 0)),
            pl.BlockSpec((1, 1, POOL_HALO, D_POOL),
                         lambda i, t: (i, jnp.minimum((t + 1) * per_tile, halo_blocks - 1), 0, 0)),
            pl.BlockSpec((1, tm, D_POOL), lambda i, t: (i, t, 0)),
            pl.BlockSpec((1, tm, D_MODEL), lambda i, t: (i, t, 0)),
            pl.BlockSpec(bands.shape, const3),
            pl.BlockSpec(wp.shape, const3),
            pl.BlockSpec((1, D_POOL), const2),
            pl.BlockSpec((D_MIX, D_MODEL), const2),
            pl.BlockSpec((1, D_MODEL), const2),
        ],
        out_specs=pl.BlockSpec((1, tm, D_MODEL), lambda i, t: (i, t, 0)),
        out_shape=jax.ShapeDtypeStruct((b, s, D_MODEL), F32),
        compiler_params=pltpu.CompilerParams(
            dimension_semantics=("arbitrary", "arbitrary"), vmem_limit_bytes=VMEM_LIMIT),
        name="outproj",
    )(ya, u, u_halo, u_halo, gp, x, bands, wp, ps, wo, fg)


def _rope_tables(seq_len):
    inv_freq = ROPE_THETA ** (-jnp.arange(0, HEAD_DIM, 2, dtype=F32) / HEAD_DIM)
    ang = jnp.arange(seq_len, dtype=F32)[:, None] * inv_freq[None, :]
    cos, sin = jnp.cos(ang), jnp.sin(ang)
    return (jnp.concatenate([cos, cos, cos, cos], axis=1),
            jnp.concatenate([-sin, -sin, sin, sin], axis=1))


def _trunk(x, norm_g, w_in, w_pool, pool_scale, w_out, final_norm_g, bias, bands):
    depth = norm_g.shape[0]
    cos, sin = _rope_tables(x.shape[1])
    fg = final_norm_g.reshape(1, D_MODEL)
    for i in range(depth):
        *qkv, ga, u, gp = _inproj(x, norm_g[i].reshape(1, D_MODEL), w_in[i], cos, sin)
        ya = _attn(qkv[0:2], qkv[2:4], qkv[4:6], ga, bias)
        x = _outproj(ya, u, gp, x, bands, w_pool[i], pool_scale[i].reshape(1, D_POOL), w_out[i], fg,
                     final=(i == depth - 1))
    return x


def kernel(x_prompt, x_sample, norm_g, w_in, w_pool, pool_scale, w_out, final_norm_g):
    order = _rope_column_order()[:D_ATTN]
    w_in = jnp.concatenate(
        [w_in[:, :, :D_ATTN][:, :, order], w_in[:, :, D_ATTN:2 * D_ATTN][:, :, order],
         w_in[:, :, 2 * D_ATTN:]], axis=2).astype(BF16)
    w_pool = w_pool.astype(BF16)
    w_out = w_out.astype(BF16)
    bias = jnp.asarray(_band_bias())
    bands = jnp.asarray(_pool_bands(), dtype=BF16)
    run = functools.partial(_trunk, norm_g=norm_g, w_in=w_in, w_pool=w_pool, pool_scale=pool_scale,
                            w_out=w_out, final_norm_g=final_norm_g, bias=bias, bands=bands)
    return (run(x_prompt), run(x_sample))
```

```python
import functools

import numpy as np
import jax
import jax.numpy as jnp
from jax import lax
from jax.experimental import pallas as pl
from jax.experimental.pallas import tpu as pltpu

D_MODEL = 1024
D_ATTN = 1024
D_POOL = 1024
D_MIX = D_ATTN + D_POOL
D_IN = 4 * D_ATTN + 2 * D_POOL
HEAD_DIM = 64
N_HEADS = D_ATTN // HEAD_DIM
LANES = 128
N_HEAD_PAIRS = D_ATTN // LANES
DILATIONS = (1, 4, 16)
VIEWS = (4, 16)
HALF = 64
POOL_WINDOWS = (2, 4, 8, 16)
POOL_GROUP_DIM = D_POOL // len(POOL_WINDOWS)
POOL_HALO = 16
POOL_BLOCK = 128
POOL_K = 256
ROPE_THETA = 10000.0
RMS_EPS = 1e-6

Q_BLOCK = 2 * HALF
K_BLOCK = Q_BLOCK + 2 * HALF
CHUNK = Q_BLOCK * DILATIONS[-1]
PAIRS_PER_STEP = 4
SCRATCH_SETS = 2
TOKEN_TILE = 512
OUT_TILE = 1024
SUB_TILE = 256
MASK_VALUE = -1e30
QUERY_SCALE = HEAD_DIM ** -0.5 * float(np.log2(np.e))
VMEM_LIMIT = 56 * 1024 * 1024

F32 = jnp.float32
BF16 = jnp.bfloat16


def _rope_column_order():
    order = []
    half = HEAD_DIM // 2
    for pair in range(N_HEAD_PAIRS):
        h0, h1 = 2 * pair * HEAD_DIM, (2 * pair + 1) * HEAD_DIM
        order += list(range(h0, h0 + half)) + list(range(h1, h1 + half))
        order += list(range(h0 + half, h0 + HEAD_DIM)) + list(range(h1 + half, h1 + HEAD_DIM))
    order = np.asarray(order, dtype=np.int32)
    return np.concatenate([order, D_ATTN + order, np.arange(2 * D_ATTN, D_IN, dtype=np.int32)])


GROUPED_BIAS = 4
GROUP_Q_ROWS = Q_BLOCK // 4
GROUP_K_ROWS = K_BLOCK // 4
GROUP_PAD = HALF // 4


def _band_bias():
    qi = np.arange(Q_BLOCK)[:, None]
    kj = np.arange(K_BLOCK)[None, :]
    out = []

    def variants(band, first_ok, last_ok):
        for first in (False, True):
            for last in (False, True):
                valid = band & (first_ok if first else True) & (last_ok if last else True)
                out.append(np.where(valid, 0.0, MASK_VALUE))

    variants(np.abs(kj - HALF - qi) <= HALF, kj >= HALF, kj < HALF + Q_BLOCK)
    qb, qw = qi // GROUP_Q_ROWS, qi % GROUP_Q_ROWS
    kb, kw = kj // GROUP_K_ROWS, kj % GROUP_K_ROWS
    rel = 4 * (kw - GROUP_PAD - qw) + (kb - qb)
    variants(np.abs(rel) <= HALF, kw >= GROUP_PAD, kw < GROUP_PAD + GROUP_Q_ROWS)
    return np.stack(out).astype(np.float32)


def _silu(g):
    return g / (1.0 + jnp.exp(-g))


def _inproj_kernel(x_ref, g_ref, w_ref, cos_ref, sin_ref,
                   q4_ref, q16_ref, k4_ref, k16_ref, v4_ref, v16_ref,
                   ga_ref, u_ref, gp_ref, nat_ref, by4_ref):
    for sub in range(TOKEN_TILE // SUB_TILE):
        _inproj_rows(sub, x_ref, g_ref, w_ref, cos_ref, sin_ref,
                     (q4_ref, q16_ref), (k4_ref, k16_ref), (v4_ref, v16_ref),
                     ga_ref, u_ref, gp_ref, nat_ref, by4_ref)


def _inproj_rows(sub, x_ref, g_ref, w_ref, cos_ref, sin_ref, q_refs, k_refs, v_refs,
                 ga_ref, u_ref, gp_ref, nat_ref, by4_ref):
    n = SUB_TILE
    rows = slice(sub * n, (sub + 1) * n)
    x = x_ref[0, rows]
    inv = lax.rsqrt(jnp.mean(x * x, axis=-1, keepdims=True) + RMS_EPS)
    h = (x * inv * g_ref[...]).astype(BF16)
    cos = cos_ref[rows]
    sin = sin_ref[rows]

    def proj(section):
        return jnp.dot(h, w_ref[:, section * D_ATTN:(section + 1) * D_ATTN],
                       preferred_element_type=F32)

    def rope(t):
        return t * cos + pltpu.roll(t, LANES // 2, axis=1) * sin

    def store_views(p, tiles, outs):
        out4, out16 = outs
        slot = sub * (N_HEAD_PAIRS // 2) + p // 2
        nat_ref[slot] = pltpu.pack_elementwise(tiles, packed_dtype=BF16)

        def halves(words):
            return [pltpu.unpack_elementwise(words, index=i, packed_dtype=BF16, unpacked_dtype=F32).astype(BF16)
                    for i in range(2)]

        for b in range(4):
            part = nat_ref[slot, pl.ds(b, n // 4, stride=4), :]
            by4_ref[slot, b * (n // 4):(b + 1) * (n // 4), :] = part
            for i, half in enumerate(halves(part)):
                out4[0, p + i, sub * (n // 4):(sub + 1) * (n // 4), b * LANES:(b + 1) * LANES] = half
        for a in range(4):
            stack = by4_ref[slot, pl.ds(a, n // 4, stride=4), :]
            for i, half in enumerate(halves(stack)):
                for b in range(4):
                    r = 4 * a + b
                    out16[0, p + i, sub * (n // 16):(sub + 1) * (n // 16), r * LANES:(r + 1) * LANES] = (
                        half[b * (n // 16):(b + 1) * (n // 16)])

    def pair_tiles(acc, p, finish):
        return [finish(acc[:, q * LANES:(q + 1) * LANES]) for q in (p, p + 1)]

    acc = proj(0)
    for p in range(0, N_HEAD_PAIRS, 2):
        store_views(p, pair_tiles(acc, p, lambda t: rope(t) * QUERY_SCALE), q_refs)
    acc = proj(1)
    for p in range(0, N_HEAD_PAIRS, 2):
        store_views(p, pair_tiles(acc, p, rope), k_refs)
    acc = proj(2)
    for p in range(0, N_HEAD_PAIRS, 2):
        store_views(p, pair_tiles(acc, p, lambda t: t), v_refs)
    acc = proj(3)
    for p in range(N_HEAD_PAIRS):
        ga_ref[0, p, rows] = _silu(acc[:, p * LANES:(p + 1) * LANES]).astype(BF16)
    gp_ref[0, rows] = _silu(proj(5)).astype(BF16)
    u_ref[0, rows] = proj(4).astype(BF16)


def _inproj(x, g, w, cos, sin):
    b, s, _ = x.shape
    tm = TOKEN_TILE

    def pair_out(d):
        shape = jax.ShapeDtypeStruct((b, N_HEAD_PAIRS, s // d, LANES * d), BF16)
        spec = pl.BlockSpec((1, N_HEAD_PAIRS, tm // d, LANES * d), lambda i, t: (i, 0, t, 0))
        return shape, spec

    flat_shape = jax.ShapeDtypeStruct((b, s, D_POOL), BF16)
    flat_spec = pl.BlockSpec((1, tm, D_POOL), lambda i, t: (i, t, 0))
    outs = [pair_out(d) for _ in range(3) for d in VIEWS] + [pair_out(1)]
    outs += [(flat_shape, flat_spec), (flat_shape, flat_spec)]
    n_slots = (N_HEAD_PAIRS // 2) * (tm // SUB_TILE)
    return pl.pallas_call(
        _inproj_kernel,
        grid=(b, s // tm),
        in_specs=[
            pl.BlockSpec((1, tm, D_MODEL), lambda i, t: (i, t, 0)),
            pl.BlockSpec((1, D_MODEL), lambda i, t: (0, 0)),
            pl.BlockSpec((D_MODEL, D_IN), lambda i, t: (0, 0), pipeline_mode=pl.Buffered(1)),
            pl.BlockSpec((tm, LANES), lambda i, t: (t, 0)),
            pl.BlockSpec((tm, LANES), lambda i, t: (t, 0)),
        ],
        out_specs=[spec for _, spec in outs],
        out_shape=[shape for shape, _ in outs],
        scratch_shapes=[pltpu.VMEM((n_slots, SUB_TILE, LANES), jnp.int32),
                        pltpu.VMEM((n_slots, SUB_TILE, LANES), jnp.int32)],
        compiler_params=pltpu.CompilerParams(
            dimension_semantics=("arbitrary", "arbitrary"), vmem_limit_bytes=VMEM_LIMIT),
        name="inproj",
    )(x, g, w, cos, sin)


def _merge(a, b):
    acc_a, m_a, l_a = a
    acc_b, m_b, l_b = b
    m = jnp.maximum(m_a, m_b)
    w_a = jnp.exp2(m_a - m)
    w_b = jnp.exp2(m_b - m)
    return w_a * acc_a + w_b * acc_b, m, w_a * l_a + w_b * l_b


def _attn_kernel(*refs, n_chunks):
    halo = n_chunks > 1
    it = iter(refs)
    q_refs = [next(it) for _ in VIEWS]
    k_refs = [[next(it) for _ in range(3 if halo else 1)] for _ in VIEWS]
    v_refs = [[next(it) for _ in range(3 if halo else 1)] for _ in VIEWS]
    ga_ref = next(it)
    bias_ref = next(it)
    if not halo:
        filler_ref = next(it)
        k_refs = [[parts[0], filler_ref, filler_ref] for parts in k_refs]
        v_refs = [[parts[0], filler_ref, filler_ref] for parts in v_refs]
    o_ref = next(it)
    scratch = tuple(it)

    ci = pl.program_id(1)
    first_chunk = (ci == 0).astype(jnp.int32)
    last_chunk = (ci == n_chunks - 1).astype(jnp.int32)

    word_lane = lax.broadcasted_iota(jnp.int32, (Q_BLOCK // 2, LANES), 1)
    all_bits = jnp.full((Q_BLOCK // 2, LANES), 0xFFFFFFFF, jnp.uint32)
    no_bits = jnp.zeros((Q_BLOCK // 2, LANES), jnp.uint32)
    in_head0 = (word_lane % (LANES // 2)) < HEAD_DIM // 2
    keep_head0 = jnp.where(in_head0, all_bits, no_bits)
    keep_head1 = jnp.where(in_head0, no_bits, all_bits)
    out_head0 = lax.broadcasted_iota(jnp.int32, (Q_BLOCK, LANES), 1) < HEAD_DIM
    ones = jnp.ones((K_BLOCK, LANES), BF16)

    def window(parts, hp, residue, lo, hi):
        lanes = slice(residue * LANES, (residue + 1) * LANES)
        rows = parts[0].shape[2]
        halo_hp = hp if halo else 0
        pieces = []
        if lo < 0:
            pieces.append(parts[1][0, halo_hp, 0, HALF + lo:HALF, lanes])
        pieces.append(parts[0][0, hp, max(lo, 0):min(hi, rows), lanes])
        if hi > rows:
            pieces.append(parts[2][0, halo_hp, 0, 0:hi - rows, lanes])
        return pieces

    def rows_of(pieces):
        return pieces[0] if len(pieces) == 1 else jnp.concatenate(pieces, axis=0)

    def block(q, kwin, vwin, bias_index):
        q = pltpu.bitcast(q, jnp.uint32)
        bias = bias_ref[bias_index]
        q2 = jnp.concatenate([pltpu.bitcast(q & keep_head0, BF16), pltpu.bitcast(q & keep_head1, BF16)], axis=0)
        s = lax.dot_general(q2, kwin, (((1,), (1,)), ((), ())), preferred_element_type=F32)
        ms, ps = [], []
        for h in range(2):
            sh = s[h * Q_BLOCK:(h + 1) * Q_BLOCK] + bias
            m = jnp.max(sh, axis=-1, keepdims=True)
            ps.append(jnp.exp2(sh - m).astype(BF16))
            ms.append(jnp.broadcast_to(m, (Q_BLOCK, LANES)))
        pv = jnp.dot(jnp.concatenate(ps, axis=0), jnp.concatenate([vwin, ones], axis=1),
                     preferred_element_type=F32)
        head0, head1 = pv[:Q_BLOCK], pv[Q_BLOCK:]
        return (jnp.where(out_head0, head0[:, :LANES], head1[:, :LANES]),
                jnp.where(out_head0, ms[0], ms[1]),
                jnp.where(out_head0, head0[:, LANES:], head1[:, LANES:]))

    def bias_index(base, is_first_block, is_last_block):
        first = first_chunk if is_first_block else 0
        last = last_chunk if is_last_block else 0
        return base + 2 * first + last

    q4_ref, q16_ref = q_refs
    (k4, k16), (v4, v16) = k_refs, v_refs
    rows16 = CHUNK // 16
    rows4 = CHUNK // 4
    for hp in range(PAIRS_PER_STEP):
        scratch_set = hp % SCRATCH_SETS
        acc_ref, m_ref, l_ref, tok_ref = scratch[4 * scratch_set:4 * scratch_set + 4]

        for r in range(16):
            a, b = divmod(r, 4)
            state = block(q16_ref[0, hp, :, r * LANES:(r + 1) * LANES],
                          rows_of(window(k16, hp, r, -HALF, rows16 + HALF)),
                          rows_of(window(v16, hp, r, -HALF, rows16 + HALF)),
                          bias_index(0, True, True))
            for ref, value in zip((acc_ref, m_ref, l_ref), state):
                ref[b, pl.ds(a, Q_BLOCK, stride=4), :] = value

        for b in range(4):
            for start in range(0, rows4, Q_BLOCK):
                rows = slice(start, start + Q_BLOCK)
                state = block(q4_ref[0, hp, rows, b * LANES:(b + 1) * LANES],
                              rows_of(window(k4, hp, b, start - HALF, start + Q_BLOCK + HALF)),
                              rows_of(window(v4, hp, b, start - HALF, start + Q_BLOCK + HALF)),
                              bias_index(0, start == 0, start == rows4 - Q_BLOCK))
                merged = _merge(state, (acc_ref[b, rows, :], m_ref[b, rows, :], l_ref[b, rows, :]))
                for ref, value in zip((acc_ref, m_ref, l_ref), merged):
                    ref[b, rows, :] = value

        for w0 in range(0, rows4, GROUP_Q_ROWS):
            lo, hi = w0 - GROUP_PAD, w0 - GROUP_PAD + GROUP_K_ROWS
            rows = slice(w0, w0 + GROUP_Q_ROWS)
            q = jnp.concatenate([q4_ref[0, hp, rows, b * LANES:(b + 1) * LANES] for b in range(4)], axis=0)
            kwin = jnp.concatenate([piece for b in range(4) for piece in window(k4, hp, b, lo, hi)], axis=0)
            vwin = jnp.concatenate([piece for b in range(4) for piece in window(v4, hp, b, lo, hi)], axis=0)
            state = block(q, kwin, vwin, bias_index(GROUPED_BIAS, w0 == 0, w0 == rows4 - GROUP_Q_ROWS))
            prev = tuple(jnp.concatenate([ref[b, rows, :] for b in range(4)], axis=0)
                         for ref in (acc_ref, m_ref, l_ref))
            acc, _, l = _merge(state, prev)
            out = acc / l
            tokens = slice(4 * w0, 4 * w0 + Q_BLOCK)
            for b in range(4):
                tok_ref[pl.ds(4 * w0 + b, GROUP_Q_ROWS, stride=4), :] = out[b * GROUP_Q_ROWS:(b + 1) * GROUP_Q_ROWS]
            o_ref[0, hp, tokens, :] = (tok_ref[tokens, :] * ga_ref[0, hp, tokens, :].astype(F32)).astype(BF16)


def _attn(q_views, k_views, v_views, ga, bias):
    b, _, rows, _ = q_views[0].shape
    s = rows * VIEWS[0]
    n_chunks = s // CHUNK
    halo = n_chunks > 1
    args, specs = [], []

    pairs = PAIRS_PER_STEP

    def center(view, d):
        args.append(view)
        specs.append(pl.BlockSpec((1, pairs, CHUNK // d, LANES * d), lambda i, c, p: (i, p, c, 0)))

    def halos(arr, d):
        blocks_per_chunk = CHUNK // d // HALF
        n_blocks = s // d // HALF
        view = arr.reshape(b, N_HEAD_PAIRS, n_blocks, HALF, LANES * d)
        shape = (1, pairs, 1, HALF, LANES * d)
        args.append(view)
        specs.append(pl.BlockSpec(
            shape, lambda i, c, p: (i, p, jnp.maximum(c * blocks_per_chunk - 1, 0), 0, 0)))
        args.append(view)
        specs.append(pl.BlockSpec(
            shape, lambda i, c, p: (i, p, jnp.minimum((c + 1) * blocks_per_chunk, n_blocks - 1), 0, 0)))

    for view, d in zip(q_views, VIEWS):
        center(view, d)
    for views in (k_views, v_views):
        for view, d in zip(views, VIEWS):
            center(view, d)
            if halo:
                halos(view, d)
    args.append(ga)
    specs.append(pl.BlockSpec((1, pairs, CHUNK, LANES), lambda i, c, p: (i, p, c, 0)))
    args.append(bias)
    specs.append(pl.BlockSpec(bias.shape, lambda i, c, p: (0, 0, 0)))
    if not halo:
        filler_shape = (1, 1, 1, HALF, LANES * VIEWS[-1])
        args.append(jnp.zeros(filler_shape, BF16))
        specs.append(pl.BlockSpec(filler_shape, lambda i, c, p: (0, 0, 0, 0, 0)))

    state = pltpu.VMEM((4, CHUNK // 4, LANES), F32)
    token_order = pltpu.VMEM((CHUNK, LANES), F32)
    return pl.pallas_call(
        functools.partial(_attn_kernel, n_chunks=n_chunks),
        grid=(b, n_chunks, N_HEAD_PAIRS // pairs),
        in_specs=specs,
        out_specs=pl.BlockSpec((1, pairs, CHUNK, LANES), lambda i, c, p: (i, p, c, 0)),
        out_shape=jax.ShapeDtypeStruct((b, N_HEAD_PAIRS, s, LANES), BF16),
        scratch_shapes=[state, state, state, token_order] * SCRATCH_SETS,
        compiler_params=pltpu.CompilerParams(
            dimension_semantics=("arbitrary", "arbitrary", "arbitrary"), vmem_limit_bytes=VMEM_LIMIT),
        name="attn",
    )(*args)


def _pool_bands():
    i = np.arange(POOL_BLOCK)[:, None]
    j = np.arange(POOL_K)[None, :]
    rel = j - POOL_HALO - i
    return np.stack([(rel >= -(w // 2)) & (rel <= w // 2 - 1) for w in POOL_WINDOWS]).astype(np.float32)


def _outproj_kernel(ya_ref, u_ref, up_ref, un_ref, gp_ref, x_ref, band_ref, wp_ref, ps_ref, wo_ref, fg_ref,
                    o_ref, *, seq_len, final):
    tm = OUT_TILE
    t = pl.program_id(1)
    n_t = pl.num_programs(1)
    halo_zero = jnp.zeros((POOL_HALO, D_POOL), BF16)
    ub = jnp.concatenate(
        [jnp.where(t > 0, up_ref[0, 0], halo_zero), u_ref[0], jnp.where(t < n_t - 1, un_ref[0, 0], halo_zero),
         jnp.zeros((POOL_K - POOL_BLOCK - 2 * POOL_HALO, D_POOL), BF16)], axis=0)

    pos = t * tm + lax.broadcasted_iota(jnp.int32, (tm, POOL_GROUP_DIM), 0)
    parts = [ya_ref[0, p] for p in range(N_HEAD_PAIRS)]
    for g, w in enumerate(POOL_WINDOWS):
        lanes = slice(g * POOL_GROUP_DIM, (g + 1) * POOL_GROUP_DIM)
        total = jnp.concatenate(
            [jnp.dot(band_ref[g], ub[r0:r0 + POOL_K, lanes], preferred_element_type=F32)
             for r0 in range(0, tm, POOL_BLOCK)], axis=0)
        count = (jnp.minimum(pos + w // 2, seq_len) - jnp.maximum(pos - w // 2, 0)).astype(F32)
        pooled = total / count - u_ref[0, :, lanes].astype(F32)
        mixed = jnp.dot(pooled.astype(BF16), wp_ref[g], preferred_element_type=F32)
        parts.append((mixed * ps_ref[:, lanes] * gp_ref[0, :, lanes].astype(F32)).astype(BF16))
    y = jnp.concatenate(parts, axis=1)
    out = x_ref[0] + jnp.dot(y, wo_ref[...], preferred_element_type=F32)
    if final:
        inv = lax.rsqrt(jnp.mean(out * out, axis=-1, keepdims=True) + RMS_EPS)
        out = out * inv * fg_ref[...]
    o_ref[0] = out


def _outproj(ya, u, gp, x, bands, wp, ps, wo, fg, final):
    b, s, _ = x.shape
    tm = OUT_TILE
    halo_blocks = s // POOL_HALO
    per_tile = tm // POOL_HALO
    u_halo = u.reshape(b, halo_blocks, POOL_HALO, D_POOL)
    const2 = lambda i, t: (0, 0)
    const3 = lambda i, t: (0, 0, 0)
    return pl.pallas_call(
        functools.partial(_outproj_kernel, seq_len=s, final=final),
        grid=(b, s // tm),
        in_specs=[
            pl.BlockSpec((1, N_HEAD_PAIRS, tm, LANES), lambda i, t: (i, 0, t, 0)),
            pl.BlockSpec((1, tm, D_POOL), lambda i, t: (i, t, 0)),
            pl.BlockSpec((1, 1, POOL_HALO, D_POOL),
                         lambda i, t: (i, jnp.maximum(t * per_tile - 1, 0), 0, 0)),
            pl.BlockSpec((1, 1, POOL_HALO, D_POOL),
                         lambda i, t: (i, jnp.minimum((t + 1) * per_tile, halo_blocks - 1), 0, 0)),
            pl.BlockSpec((1, tm, D_POOL), lambda i, t: (i, t, 0)),
            pl.BlockSpec((1, tm, D_MODEL), lambda i, t: (i, t, 0)),
            pl.BlockSpec(bands.shape, const3),
            pl.BlockSpec(wp.shape, const3),
            pl.BlockSpec((1, D_POOL), const2),
            pl.BlockSpec((D_MIX, D_MODEL), const2),
            pl.BlockSpec((1, D_MODEL), const2),
        ],
        out_specs=pl.BlockSpec((1, tm, D_MODEL), lambda i, t: (i, t, 0)),
        out_shape=jax.ShapeDtypeStruct((b, s, D_MODEL), F32),
        compiler_params=pltpu.CompilerParams(
            dimension_semantics=("arbitrary", "arbitrary"), vmem_limit_bytes=VMEM_LIMIT),
        name="outproj",
    )(ya, u, u_halo, u_halo, gp, x, bands, wp, ps, wo, fg)


def _rope_tables(seq_len):
    inv_freq = ROPE_THETA ** (-jnp.arange(0, HEAD_DIM, 2, dtype=F32) / HEAD_DIM)
    ang = jnp.arange(seq_len, dtype=F32)[:, None] * inv_freq[None, :]
    cos, sin = jnp.cos(ang), jnp.sin(ang)
    return (jnp.concatenate([cos, cos, cos, cos], axis=1),
            jnp.concatenate([-sin, -sin, sin, sin], axis=1))


def _trunk(x, norm_g, w_in, w_pool, pool_scale, w_out, final_norm_g, bias, bands):
    depth = norm_g.shape[0]
    cos, sin = _rope_tables(x.shape[1])
    fg = final_norm_g.reshape(1, D_MODEL)
    for i in range(depth):
        *qkv, ga, u, gp = _inproj(x, norm_g[i].reshape(1, D_MODEL), w_in[i], cos, sin)
        ya = _attn(qkv[0:2], qkv[2:4], qkv[4:6], ga, bias)
        x = _outproj(ya, u, gp, x, bands, w_pool[i], pool_scale[i].reshape(1, D_POOL), w_out[i], fg,
                     final=(i == depth - 1))
    return x


def kernel(x_prompt, x_sample, norm_g, w_in, w_pool, pool_scale, w_out, final_norm_g):
    order = _rope_column_order()[:D_ATTN]
    w_in = jnp.concatenate(
        [w_in[:, :, :D_ATTN][:, :, order], w_in[:, :, D_ATTN:2 * D_ATTN][:, :, order],
         w_in[:, :, 2 * D_ATTN:]], axis=2).astype(BF16)
    w_pool = w_pool.astype(BF16)
    w_out = w_out.astype(BF16)
    bias = jnp.asarray(_band_bias())
    bands = jnp.asarray(_pool_bands(), dtype=BF16)
    run = functools.partial(_trunk, norm_g=norm_g, w_in=w_in, w_pool=w_pool, pool_scale=pool_scale,
                            w_out=w_out, final_norm_g=final_norm_g, bias=bias, bands=bands)
    return (run(x_prompt), run(x_sample))
```

```python
import functools

import numpy as np
import jax
import jax.numpy as jnp
from jax import lax
from jax.experimental import pallas as pl
from jax.experimental.pallas import tpu as pltpu

D_MODEL = 1024
D_ATTN = 1024
D_POOL = 1024
D_MIX = D_ATTN + D_POOL
D_IN = 4 * D_ATTN + 2 * D_POOL
HEAD_DIM = 64
N_HEADS = D_ATTN // HEAD_DIM
LANES = 128
N_HEAD_PAIRS = D_ATTN // LANES
DILATIONS = (1, 4, 16)
VIEWS = (4, 16)
HALF = 64
POOL_WINDOWS = (2, 4, 8, 16)
POOL_GROUP_DIM = D_POOL // len(POOL_WINDOWS)
POOL_HALO = 16
POOL_BLOCK = 128
POOL_K = 256
ROPE_THETA = 10000.0
RMS_EPS = 1e-6

Q_BLOCK = 2 * HALF
K_BLOCK = Q_BLOCK + 2 * HALF
CHUNK = Q_BLOCK * DILATIONS[-1]
PAIRS_PER_STEP = 4
SCRATCH_SETS = 2
TOKEN_TILE = 512
OUT_TILE = 1024
SUB_TILE = 256
MASK_VALUE = -1e30
QUERY_SCALE = HEAD_DIM ** -0.5 * float(np.log2(np.e))
VMEM_LIMIT = 56 * 1024 * 1024

F32 = jnp.float32
BF16 = jnp.bfloat16


def _rope_column_order():
    order = []
    half = HEAD_DIM // 2
    for pair in range(N_HEAD_PAIRS):
        h0, h1 = 2 * pair * HEAD_DIM, (2 * pair + 1) * HEAD_DIM
        order += list(range(h0, h0 + half)) + list(range(h1, h1 + half))
        order += list(range(h0 + half, h0 + HEAD_DIM)) + list(range(h1 + half, h1 + HEAD_DIM))
    order = np.asarray(order, dtype=np.int32)
    return np.concatenate([order, D_ATTN + order, np.arange(2 * D_ATTN, D_IN, dtype=np.int32)])


GROUPED_BIAS = 4
GROUP_Q_ROWS = Q_BLOCK // 4
GROUP_K_ROWS = K_BLOCK // 4
GROUP_PAD = HALF // 4


def _band_bias():
    qi = np.arange(Q_BLOCK)[:, None]
    kj = np.arange(K_BLOCK)[None, :]
    out = []

    def variants(band, first_ok, last_ok):
        for first in (False, True):
            for last in (False, True):
                valid = band & (first_ok if first else True) & (last_ok if last else True)
                out.append(np.where(valid, 0.0, MASK_VALUE))

    variants(np.abs(kj - HALF - qi) <= HALF, kj >= HALF, kj < HALF + Q_BLOCK)
    qb, qw = qi // GROUP_Q_ROWS, qi % GROUP_Q_ROWS
    kb, kw = kj // GROUP_K_ROWS, kj % GROUP_K_ROWS
    rel = 4 * (kw - GROUP_PAD - qw) + (kb - qb)
    variants(np.abs(rel) <= HALF, kw >= GROUP_PAD, kw < GROUP_PAD + GROUP_Q_ROWS)
    return np.stack(out).astype(np.float32)


def _silu(g):
    return g / (1.0 + jnp.exp(-g))


def _inproj_kernel(x_ref, g_ref, w_ref, cos_ref, sin_ref,
                   q4_ref, q16_ref, k4_ref, k16_ref, v4_ref, v16_ref,
                   ga_ref, u_ref, gp_ref, nat_ref, by4_ref):
    for sub in range(TOKEN_TILE // SUB_TILE):
        _inproj_rows(sub, x_ref, g_ref, w_ref, cos_ref, sin_ref,
                     (q4_ref, q16_ref), (k4_ref, k16_ref), (v4_ref, v16_ref),
                     ga_ref, u_ref, gp_ref, nat_ref, by4_ref)


def _inproj_rows(sub, x_ref, g_ref, w_ref, cos_ref, sin_ref, q_refs, k_refs, v_refs,
                 ga_ref, u_ref, gp_ref, nat_ref, by4_ref):
    n = SUB_TILE
    rows = slice(sub * n, (sub + 1) * n)
    x = x_ref[0, rows]
    inv = lax.rsqrt(jnp.mean(x * x, axis=-1, keepdims=True) + RMS_EPS)
    h = (x * inv * g_ref[...]).astype(BF16)
    cos = cos_ref[rows]
    sin = sin_ref[rows]

    def proj(section):
        return jnp.dot(h, w_ref[:, section * D_ATTN:(section + 1) * D_ATTN],
                       preferred_element_type=F32)

    def rope(t):
        return t * cos + pltpu.roll(t, LANES // 2, axis=1) * sin

    def store_views(p, tiles, outs):
        out4, out16 = outs
        slot = sub * (N_HEAD_PAIRS // 2) + p // 2
        nat_ref[slot] = pltpu.pack_elementwise(tiles, packed_dtype=BF16)

        def halves(words):
            return [pltpu.unpack_elementwise(words, index=i, packed_dtype=BF16, unpacked_dtype=F32).astype(BF16)
                    for i in range(2)]

        for b in range(4):
            part = nat_ref[slot, pl.ds(b, n // 4, stride=4), :]
            by4_ref[slot, b * (n // 4):(b + 1) * (n // 4), :] = part
            for i, half in enumerate(halves(part)):
                out4[0, p + i, sub * (n // 4):(sub + 1) * (n // 4), b * LANES:(b + 1) * LANES] = half
        for a in range(4):
            stack = by4_ref[slot, pl.ds(a, n // 4, stride=4), :]
            for i, half in enumerate(halves(stack)):
                for b in range(4):
                    r = 4 * a + b
                    out16[0, p + i, sub * (n // 16):(sub + 1) * (n // 16), r * LANES:(r + 1) * LANES] = (
                        half[b * (n // 16):(b + 1) * (n // 16)])

    def pair_tiles(acc, p, finish):
        return [finish(acc[:, q * LANES:(q + 1) * LANES]) for q in (p, p + 1)]

    acc = proj(0)
    for p in range(0, N_HEAD_PAIRS, 2):
        store_views(p, pair_tiles(acc, p, lambda t: rope(t) * QUERY_SCALE), q_refs)
    acc = proj(1)
    for p in range(0, N_HEAD_PAIRS, 2):
        store_views(p, pair_tiles(acc, p, rope), k_refs)
    acc = proj(2)
    for p in range(0, N_HEAD_PAIRS, 2):
        store_views(p, pair_tiles(acc, p, lambda t: t), v_refs)
    acc = proj(3)
    for p in range(N_HEAD_PAIRS):
        ga_ref[0, p, rows] = _silu(acc[:, p * LANES:(p + 1) * LANES]).astype(BF16)
    gp_ref[0, rows] = _silu(proj(5)).astype(BF16)
    u_ref[0, rows] = proj(4).astype(BF16)


def _inproj(x, g, w, layer, cos, sin):
    b, s, _ = x.shape
    tm = TOKEN_TILE

    def pair_out(d):
        shape = jax.ShapeDtypeStruct((b, N_HEAD_PAIRS, s // d, LANES * d), BF16)
        spec = pl.BlockSpec((1, N_HEAD_PAIRS, tm // d, LANES * d), lambda i, t: (i, 0, t, 0))
        return shape, spec

    flat_shape = jax.ShapeDtypeStruct((b, s, D_POOL), BF16)
    flat_spec = pl.BlockSpec((1, tm, D_POOL), lambda i, t: (i, t, 0))
    outs = [pair_out(d) for _ in range(3) for d in VIEWS] + [pair_out(1)]
    outs += [(flat_shape, flat_spec), (flat_shape, flat_spec)]
    n_slots = (N_HEAD_PAIRS // 2) * (tm // SUB_TILE)
    return pl.pallas_call(
        _inproj_kernel,
        grid=(b, s // tm),
        in_specs=[
            pl.BlockSpec((1, tm, D_MODEL), lambda i, t: (i, t, 0)),
            pl.BlockSpec((1, D_MODEL), lambda i, t: (0, 0)),
            pl.BlockSpec((None, D_MODEL, D_IN), lambda i, t: (layer, 0, 0), pipeline_mode=pl.Buffered(1)),
            pl.BlockSpec((tm, LANES), lambda i, t: (t, 0)),
            pl.BlockSpec((tm, LANES), lambda i, t: (t, 0)),
        ],
        out_specs=[spec for _, spec in outs],
        out_shape=[shape for shape, _ in outs],
        scratch_shapes=[pltpu.VMEM((n_slots, SUB_TILE, LANES), jnp.int32),
                        pltpu.VMEM((n_slots, SUB_TILE, LANES), jnp.int32)],
        compiler_params=pltpu.CompilerParams(
            dimension_semantics=("arbitrary", "arbitrary"), vmem_limit_bytes=VMEM_LIMIT),
        name="inproj",
    )(x, g, w, cos, sin)


def _merge(a, b):
    acc_a, m_a, l_a = a
    acc_b, m_b, l_b = b
    m = jnp.maximum(m_a, m_b)
    w_a = jnp.exp2(m_a - m)
    w_b = jnp.exp2(m_b - m)
    return w_a * acc_a + w_b * acc_b, m, w_a * l_a + w_b * l_b


def _attn_kernel(*refs, n_chunks):
    halo = n_chunks > 1
    it = iter(refs)
    q_refs = [next(it) for _ in VIEWS]
    k_refs = [[next(it) for _ in range(3 if halo else 1)] for _ in VIEWS]
    v_refs = [[next(it) for _ in range(3 if halo else 1)] for _ in VIEWS]
    ga_ref = next(it)
    bias_ref = next(it)
    if not halo:
        filler_ref = next(it)
        k_refs = [[parts[0], filler_ref, filler_ref] for parts in k_refs]
        v_refs = [[parts[0], filler_ref, filler_ref] for parts in v_refs]
    o_ref = next(it)
    scratch = tuple(it)

    ci = pl.program_id(1)
    first_chunk = (ci == 0).astype(jnp.int32)
    last_chunk = (ci == n_chunks - 1).astype(jnp.int32)

    word_lane = lax.broadcasted_iota(jnp.int32, (Q_BLOCK // 2, LANES), 1)
    all_bits = jnp.full((Q_BLOCK // 2, LANES), 0xFFFFFFFF, jnp.uint32)
    no_bits = jnp.zeros((Q_BLOCK // 2, LANES), jnp.uint32)
    in_head0 = (word_lane % (LANES // 2)) < HEAD_DIM // 2
    keep_head0 = jnp.where(in_head0, all_bits, no_bits)
    keep_head1 = jnp.where(in_head0, no_bits, all_bits)
    out_head0 = lax.broadcasted_iota(jnp.int32, (Q_BLOCK, LANES), 1) < HEAD_DIM
    ones = jnp.ones((K_BLOCK, LANES), BF16)

    def window(parts, hp, residue, lo, hi):
        lanes = slice(residue * LANES, (residue + 1) * LANES)
        rows = parts[0].shape[2]
        halo_hp = hp if halo else 0
        pieces = []
        if lo < 0:
            pieces.append(parts[1][0, halo_hp, 0, HALF + lo:HALF, lanes])
        pieces.append(parts[0][0, hp, max(lo, 0):min(hi, rows), lanes])
        if hi > rows:
            pieces.append(parts[2][0, halo_hp, 0, 0:hi - rows, lanes])
        return pieces

    def rows_of(pieces):
        return pieces[0] if len(pieces) == 1 else jnp.concatenate(pieces, axis=0)

    def block(q, kwin, vwin, bias_index):
        q = pltpu.bitcast(q, jnp.uint32)
        bias = bias_ref[bias_index]
        q2 = jnp.concatenate([pltpu.bitcast(q & keep_head0, BF16), pltpu.bitcast(q & keep_head1, BF16)], axis=0)
        s = lax.dot_general(q2, kwin, (((1,), (1,)), ((), ())), preferred_element_type=F32)
        ms, ps = [], []
        for h in range(2):
            sh = s[h * Q_BLOCK:(h + 1) * Q_BLOCK] + bias
            m = jnp.max(sh, axis=-1, keepdims=True)
            ps.append(jnp.exp2(sh - m).astype(BF16))
            ms.append(jnp.broadcast_to(m, (Q_BLOCK, LANES)))
        pv = jnp.dot(jnp.concatenate(ps, axis=0), jnp.concatenate([vwin, ones], axis=1),
                     preferred_element_type=F32)
        head0, head1 = pv[:Q_BLOCK], pv[Q_BLOCK:]
        return (jnp.where(out_head0, head0[:, :LANES], head1[:, :LANES]),
                jnp.where(out_head0, ms[0], ms[1]),
                jnp.where(out_head0, head0[:, LANES:], head1[:, LANES:]))

    def bias_index(base, is_first_block, is_last_block):
        first = first_chunk if is_first_block else 0
        last = last_chunk if is_last_block else 0
        return base + 2 * first + last

    q4_ref, q16_ref = q_refs
    (k4, k16), (v4, v16) = k_refs, v_refs
    rows16 = CHUNK // 16
    rows4 = CHUNK // 4
    for hp in range(PAIRS_PER_STEP):
        scratch_set = hp % SCRATCH_SETS
        acc_ref, m_ref, l_ref, tok_ref = scratch[4 * scratch_set:4 * scratch_set + 4]

        for r in range(16):
            a, b = divmod(r, 4)
            state = block(q16_ref[0, hp, :, r * LANES:(r + 1) * LANES],
                          rows_of(window(k16, hp, r, -HALF, rows16 + HALF)),
                          rows_of(window(v16, hp, r, -HALF, rows16 + HALF)),
                          bias_index(0, True, True))
            for ref, value in zip((acc_ref, m_ref, l_ref), state):
                ref[b, pl.ds(a, Q_BLOCK, stride=4), :] = value

        for b in range(4):
            for start in range(0, rows4, Q_BLOCK):
                rows = slice(start, start + Q_BLOCK)
                state = block(q4_ref[0, hp, rows, b * LANES:(b + 1) * LANES],
                              rows_of(window(k4, hp, b, start - HALF, start + Q_BLOCK + HALF)),
                              rows_of(window(v4, hp, b, start - HALF, start + Q_BLOCK + HALF)),
                              bias_index(0, start == 0, start == rows4 - Q_BLOCK))
                merged = _merge(state, (acc_ref[b, rows, :], m_ref[b, rows, :], l_ref[b, rows, :]))
                for ref, value in zip((acc_ref, m_ref, l_ref), merged):
                    ref[b, rows, :] = value

        for w0 in range(0, rows4, GROUP_Q_ROWS):
            lo, hi = w0 - GROUP_PAD, w0 - GROUP_PAD + GROUP_K_ROWS
            rows = slice(w0, w0 + GROUP_Q_ROWS)
            q = jnp.concatenate([q4_ref[0, hp, rows, b * LANES:(b + 1) * LANES] for b in range(4)], axis=0)
            kwin = jnp.concatenate([piece for b in range(4) for piece in window(k4, hp, b, lo, hi)], axis=0)
            vwin = jnp.concatenate([piece for b in range(4) for piece in window(v4, hp, b, lo, hi)], axis=0)
            state = block(q, kwin, vwin, bias_index(GROUPED_BIAS, w0 == 0, w0 == rows4 - GROUP_Q_ROWS))
            prev = tuple(jnp.concatenate([ref[b, rows, :] for b in range(4)], axis=0)
                         for ref in (acc_ref, m_ref, l_ref))
            acc, _, l = _merge(state, prev)
            out = acc / l
            tokens = slice(4 * w0, 4 * w0 + Q_BLOCK)
            for b in range(4):
                tok_ref[pl.ds(4 * w0 + b, GROUP_Q_ROWS, stride=4), :] = out[b * GROUP_Q_ROWS:(b + 1) * GROUP_Q_ROWS]
            o_ref[0, hp, tokens, :] = (tok_ref[tokens, :] * ga_ref[0, hp, tokens, :].astype(F32)).astype(BF16)


def _attn(q_views, k_views, v_views, ga, bias):
    b, _, rows, _ = q_views[0].shape
    s = rows * VIEWS[0]
    n_chunks = s // CHUNK
    halo = n_chunks > 1
    args, specs = [], []

    pairs = PAIRS_PER_STEP

    def center(view, d):
        args.append(view)
        specs.append(pl.BlockSpec((1, pairs, CHUNK // d, LANES * d), lambda i, c, p: (i, p, c, 0)))

    def halos(arr, d):
        blocks_per_chunk = CHUNK // d // HALF
        n_blocks = s // d // HALF
        view = arr.reshape(b, N_HEAD_PAIRS, n_blocks, HALF, LANES * d)
        shape = (1, pairs, 1, HALF, LANES * d)
        args.append(view)
        specs.append(pl.BlockSpec(
            shape, lambda i, c, p: (i, p, jnp.maximum(c * blocks_per_chunk - 1, 0), 0, 0)))
        args.append(view)
        specs.append(pl.BlockSpec(
            shape, lambda i, c, p: (i, p, jnp.minimum((c + 1) * blocks_per_chunk, n_blocks - 1), 0, 0)))

    for view, d in zip(q_views, VIEWS):
        center(view, d)
    for views in (k_views, v_views):
        for view, d in zip(views, VIEWS):
            center(view, d)
            if halo:
                halos(view, d)
    args.append(ga)
    specs.append(pl.BlockSpec((1, pairs, CHUNK, LANES), lambda i, c, p: (i, p, c, 0)))
    args.append(bias)
    specs.append(pl.BlockSpec(bias.shape, lambda i, c, p: (0, 0, 0)))
    if not halo:
        filler_shape = (1, 1, 1, HALF, LANES * VIEWS[-1])
        args.append(jnp.zeros(filler_shape, BF16))
        specs.append(pl.BlockSpec(filler_shape, lambda i, c, p: (0, 0, 0, 0, 0)))

    state = pltpu.VMEM((4, CHUNK // 4, LANES), F32)
    token_order = pltpu.VMEM((CHUNK, LANES), F32)
    return pl.pallas_call(
        functools.partial(_attn_kernel, n_chunks=n_chunks),
        grid=(b, n_chunks, N_HEAD_PAIRS // pairs),
        in_specs=specs,
        out_specs=pl.BlockSpec((1, pairs, CHUNK, LANES), lambda i, c, p: (i, p, c, 0)),
        out_shape=jax.ShapeDtypeStruct((b, N_HEAD_PAIRS, s, LANES), BF16),
        scratch_shapes=[state, state, state, token_order] * SCRATCH_SETS,
        compiler_params=pltpu.CompilerParams(
            dimension_semantics=("arbitrary", "arbitrary", "arbitrary"), vmem_limit_bytes=VMEM_LIMIT),
        name="attn",
    )(*args)


def _pool_bands():
    i = np.arange(POOL_BLOCK)[:, None]
    j = np.arange(POOL_K)[None, :]
    rel = j - POOL_HALO - i
    return np.stack([(rel >= -(w // 2)) & (rel <= w // 2 - 1) for w in POOL_WINDOWS]).astype(np.float32)


def _outproj_kernel(ya_ref, u_ref, up_ref, un_ref, gp_ref, x_ref, band_ref, wp_ref, ps_ref, wo_ref, fg_ref,
                    o_ref, *, seq_len, final):
    tm = OUT_TILE
    t = pl.program_id(1)
    n_t = pl.num_programs(1)
    halo_zero = jnp.zeros((POOL_HALO, D_POOL), BF16)
    ub = jnp.concatenate(
        [jnp.where(t > 0, up_ref[0, 0], halo_zero), u_ref[0], jnp.where(t < n_t - 1, un_ref[0, 0], halo_zero),
         jnp.zeros((POOL_K - POOL_BLOCK - 2 * POOL_HALO, D_POOL), BF16)], axis=0)

    pos = t * tm + lax.broadcasted_iota(jnp.int32, (tm, POOL_GROUP_DIM), 0)
    parts = [ya_ref[0, p] for p in range(N_HEAD_PAIRS)]
    for g, w in enumerate(POOL_WINDOWS):
        lanes = slice(g * POOL_GROUP_DIM, (g + 1) * POOL_GROUP_DIM)
        total = jnp.concatenate(
            [jnp.dot(band_ref[g], ub[r0:r0 + POOL_K, lanes], preferred_element_type=F32)
             for r0 in range(0, tm, POOL_BLOCK)], axis=0)
        count = (jnp.minimum(pos + w // 2, seq_len) - jnp.maximum(pos - w // 2, 0)).astype(F32)
        pooled = total / count - u_ref[0, :, lanes].astype(F32)
        mixed = jnp.dot(pooled.astype(BF16), wp_ref[g], preferred_element_type=F32)
        parts.append((mixed * ps_ref[:, lanes] * gp_ref[0, :, lanes].astype(F32)).astype(BF16))
    y = jnp.concatenate(parts, axis=1)
    out = x_ref[0] + jnp.dot(y, wo_ref[...], preferred_element_type=F32)
    if final:
        inv = lax.rsqrt(jnp.mean(out * out, axis=-1, keepdims=True) + RMS_EPS)
        out = out * inv * fg_ref[...]
    o_ref[0] = out


def _outproj(ya, u, gp, x, bands, wp, ps, wo, layer, fg, final):
    b, s, _ = x.shape
    tm = OUT_TILE
    halo_blocks = s // POOL_HALO
    per_tile = tm // POOL_HALO
    u_halo = u.reshape(b, halo_blocks, POOL_HALO, D_POOL)
    const2 = lambda i, t: (0, 0)
    const3 = lambda i, t: (0, 0, 0)
    return pl.pallas_call(
        functools.partial(_outproj_kernel, seq_len=s, final=final),
        grid=(b, s // tm),
        in_specs=[
            pl.BlockSpec((1, N_HEAD_PAIRS, tm, LANES), lambda i, t: (i, 0, t, 0)),
            pl.BlockSpec((1, tm, D_POOL), lambda i, t: (i, t, 0)),
            pl.BlockSpec((1, 1, POOL_HALO, D_POOL),
                         lambda i, t: (i, jnp.maximum(t * per_tile - 1, 0), 0, 0)),
            pl.BlockSpec((1, 1, POOL_HALO, D_POOL),
                         lambda i, t: (i, jnp.minimum((t + 1) * per_tile, halo_blocks - 1), 0, 0)),
            pl.BlockSpec((1, tm, D_POOL), lambda i, t: (i, t, 0)),
            pl.BlockSpec((1, tm, D_MODEL), lambda i, t: (i, t, 0)),
            pl.BlockSpec(bands.shape, const3),
            pl.BlockSpec(wp.shape, const3),
            pl.BlockSpec((1, D_POOL), const2),
            pl.BlockSpec((None, D_MIX, D_MODEL), lambda i, t: (layer, 0, 0)),
            pl.BlockSpec((1, D_MODEL), const2),
        ],
        out_specs=pl.BlockSpec((1, tm, D_MODEL), lambda i, t: (i, t, 0)),
        out_shape=jax.ShapeDtypeStruct((b, s, D_MODEL), F32),
        compiler_params=pltpu.CompilerParams(
            dimension_semantics=("arbitrary", "arbitrary"), vmem_limit_bytes=VMEM_LIMIT),
        name="outproj",
    )(ya, u, u_halo, u_halo, gp, x, bands, wp, ps, wo, fg)


def _rope_tables(seq_len):
    inv_freq = ROPE_THETA ** (-jnp.arange(0, HEAD_DIM, 2, dtype=F32) / HEAD_DIM)
    ang = jnp.arange(seq_len, dtype=F32)[:, None] * inv_freq[None, :]
    cos, sin = jnp.cos(ang), jnp.sin(ang)
    return (jnp.concatenate([cos, cos, cos, cos], axis=1),
            jnp.concatenate([-sin, -sin, sin, sin], axis=1))


def _trunk(x, norm_g, w_in, w_pool, pool_scale, w_out, final_norm_g, bias, bands):
    depth = norm_g.shape[0]
    cos, sin = _rope_tables(x.shape[1])
    fg = final_norm_g.reshape(1, D_MODEL)
    for i in range(depth):
        *qkv, ga, u, gp = _inproj(x, norm_g[i].reshape(1, D_MODEL), w_in, i, cos, sin)
        ya = _attn(qkv[0:2], qkv[2:4], qkv[4:6], ga, bias)
        x = _outproj(ya, u, gp, x, bands, w_pool[i], pool_scale[i].reshape(1, D_POOL), w_out, i, fg,
                     final=(i == depth - 1))
    return x


def kernel(x_prompt, x_sample, norm_g, w_in, w_pool, pool_scale, w_out, final_norm_g):
    order = _rope_column_order()[:D_ATTN]
    w_in = jnp.concatenate(
        [w_in[:, :, :D_ATTN][:, :, order], w_in[:, :, D_ATTN:2 * D_ATTN][:, :, order],
         w_in[:, :, 2 * D_ATTN:]], axis=2).astype(BF16)
    w_pool = w_pool.astype(BF16)
    w_out = w_out.astype(BF16)
    bias = jnp.asarray(_band_bias())
    bands = jnp.asarray(_pool_bands(), dtype=BF16)
    run = functools.partial(_trunk, norm_g=norm_g, w_in=w_in, w_pool=w_pool, pool_scale=pool_scale,
                            w_out=w_out, final_norm_g=final_norm_g, bias=bias, bands=bands)
    return (run(x_prompt), run(x_sample))
```

```python
import functools

import numpy as np
import jax
import jax.numpy as jnp
from jax import lax
from jax.experimental import pallas as pl
from jax.experimental.pallas import tpu as pltpu

D_MODEL = 1024
D_ATTN = 1024
D_POOL = 1024
D_MIX = D_ATTN + D_POOL
D_IN = 4 * D_ATTN + 2 * D_POOL
HEAD_DIM = 64
N_HEADS = D_ATTN // HEAD_DIM
LANES = 128
N_HEAD_PAIRS = D_ATTN // LANES
DILATIONS = (1, 4, 16)
VIEWS = (4, 16)
HALF = 64
POOL_WINDOWS = (2, 4, 8, 16)
POOL_GROUP_DIM = D_POOL // len(POOL_WINDOWS)
POOL_HALO = 16
POOL_BLOCK = 128
POOL_K = 256
ROPE_THETA = 10000.0
RMS_EPS = 1e-6

Q_BLOCK = 2 * HALF
K_BLOCK = Q_BLOCK + 2 * HALF
CHUNK = Q_BLOCK * DILATIONS[-1]
PAIRS_PER_STEP = 4
SCRATCH_SETS = 2
TOKEN_TILE = 512
OUT_TILE = 1024
SUB_TILE = 256
MASK_VALUE = -1e30
QUERY_SCALE = HEAD_DIM ** -0.5 * float(np.log2(np.e))
VMEM_LIMIT = 56 * 1024 * 1024

F32 = jnp.float32
BF16 = jnp.bfloat16


def _rope_column_order():
    order = []
    half = HEAD_DIM // 2
    for pair in range(N_HEAD_PAIRS):
        h0, h1 = 2 * pair * HEAD_DIM, (2 * pair + 1) * HEAD_DIM
        order += list(range(h0, h0 + half)) + list(range(h1, h1 + half))
        order += list(range(h0 + half, h0 + HEAD_DIM)) + list(range(h1 + half, h1 + HEAD_DIM))
    order = np.asarray(order, dtype=np.int32)
    return np.concatenate([order, D_ATTN + order, np.arange(2 * D_ATTN, D_IN, dtype=np.int32)])


GROUPED_BIAS = 4
GROUP_Q_ROWS = Q_BLOCK // 4
GROUP_K_ROWS = K_BLOCK // 4
GROUP_PAD = HALF // 4


def _band_bias():
    qi = np.arange(Q_BLOCK)[:, None]
    kj = np.arange(K_BLOCK)[None, :]
    out = []

    def variants(band, first_ok, last_ok):
        for first in (False, True):
            for last in (False, True):
                valid = band & (first_ok if first else True) & (last_ok if last else True)
                out.append(np.where(valid, 0.0, MASK_VALUE))

    variants(np.abs(kj - HALF - qi) <= HALF, kj >= HALF, kj < HALF + Q_BLOCK)
    qb, qw = qi // GROUP_Q_ROWS, qi % GROUP_Q_ROWS
    kb, kw = kj // GROUP_K_ROWS, kj % GROUP_K_ROWS
    rel = 4 * (kw - GROUP_PAD - qw) + (kb - qb)
    variants(np.abs(rel) <= HALF, kw >= GROUP_PAD, kw < GROUP_PAD + GROUP_Q_ROWS)
    return np.stack(out).astype(np.float32)


def _silu(g):
    return g / (1.0 + jnp.exp(-g))


def _inproj_kernel(x_ref, g_ref, wq_ref, wk_ref, wrest_ref, cos_ref, sin_ref,
                   q4_ref, q16_ref, k4_ref, k16_ref, v4_ref, v16_ref,
                   ga_ref, u_ref, gp_ref, nat_ref, by4_ref):
    for sub in range(TOKEN_TILE // SUB_TILE):
        _inproj_rows(sub, x_ref, g_ref, (wq_ref, wk_ref, wrest_ref), cos_ref, sin_ref,
                     (q4_ref, q16_ref), (k4_ref, k16_ref), (v4_ref, v16_ref),
                     ga_ref, u_ref, gp_ref, nat_ref, by4_ref)


def _inproj_rows(sub, x_ref, g_ref, w_refs, cos_ref, sin_ref, q_refs, k_refs, v_refs,
                 ga_ref, u_ref, gp_ref, nat_ref, by4_ref):
    n = SUB_TILE
    rows = slice(sub * n, (sub + 1) * n)
    x = x_ref[0, rows]
    inv = lax.rsqrt(jnp.mean(x * x, axis=-1, keepdims=True) + RMS_EPS)
    h = (x * inv * g_ref[...]).astype(BF16)
    cos = cos_ref[rows]
    sin = sin_ref[rows]

    def proj(section):
        if section < 2:
            w = w_refs[section][...]
        else:
            w = w_refs[2][:, (section - 2) * D_ATTN:(section - 1) * D_ATTN]
        return jnp.dot(h, w, preferred_element_type=F32)

    def rope(t):
        return t * cos + pltpu.roll(t, LANES // 2, axis=1) * sin

    def store_views(p, tiles, outs):
        out4, out16 = outs
        slot = sub * (N_HEAD_PAIRS // 2) + p // 2
        nat_ref[slot] = pltpu.pack_elementwise(tiles, packed_dtype=BF16)

        def halves(words):
            return [pltpu.unpack_elementwise(words, index=i, packed_dtype=BF16, unpacked_dtype=F32).astype(BF16)
                    for i in range(2)]

        for b in range(4):
            part = nat_ref[slot, pl.ds(b, n // 4, stride=4), :]
            by4_ref[slot, b * (n // 4):(b + 1) * (n // 4), :] = part
            for i, half in enumerate(halves(part)):
                out4[0, p + i, sub * (n // 4):(sub + 1) * (n // 4), b * LANES:(b + 1) * LANES] = half
        for a in range(4):
            stack = by4_ref[slot, pl.ds(a, n // 4, stride=4), :]
            for i, half in enumerate(halves(stack)):
                for b in range(4):
                    r = 4 * a + b
                    out16[0, p + i, sub * (n // 16):(sub + 1) * (n // 16), r * LANES:(r + 1) * LANES] = (
                        half[b * (n // 16):(b + 1) * (n // 16)])

    def pair_tiles(acc, p, finish):
        return [finish(acc[:, q * LANES:(q + 1) * LANES]) for q in (p, p + 1)]

    acc = proj(0)
    for p in range(0, N_HEAD_PAIRS, 2):
        store_views(p, pair_tiles(acc, p, lambda t: rope(t) * QUERY_SCALE), q_refs)
    acc = proj(1)
    for p in range(0, N_HEAD_PAIRS, 2):
        store_views(p, pair_tiles(acc, p, rope), k_refs)
    acc = proj(2)
    for p in range(0, N_HEAD_PAIRS, 2):
        store_views(p, pair_tiles(acc, p, lambda t: t), v_refs)
    acc = proj(3)
    for p in range(N_HEAD_PAIRS):
        ga_ref[0, p, rows] = _silu(acc[:, p * LANES:(p + 1) * LANES]).astype(BF16)
    gp_ref[0, rows] = _silu(proj(5)).astype(BF16)
    u_ref[0, rows] = proj(4).astype(BF16)


def _inproj(x, g, weights, layer, cos, sin):
    b, s, _ = x.shape
    tm = TOKEN_TILE

    def weight_spec(w):
        return pl.BlockSpec((None,) + w.shape[1:], lambda i, t: (layer, 0, 0), pipeline_mode=pl.Buffered(1))

    def pair_out(d):
        shape = jax.ShapeDtypeStruct((b, N_HEAD_PAIRS, s // d, LANES * d), BF16)
        spec = pl.BlockSpec((1, N_HEAD_PAIRS, tm // d, LANES * d), lambda i, t: (i, 0, t, 0))
        return shape, spec

    flat_shape = jax.ShapeDtypeStruct((b, s, D_POOL), BF16)
    flat_spec = pl.BlockSpec((1, tm, D_POOL), lambda i, t: (i, t, 0))
    outs = [pair_out(d) for _ in range(3) for d in VIEWS] + [pair_out(1)]
    outs += [(flat_shape, flat_spec), (flat_shape, flat_spec)]
    n_slots = (N_HEAD_PAIRS // 2) * (tm // SUB_TILE)
    return pl.pallas_call(
        _inproj_kernel,
        grid=(b, s // tm),
        in_specs=[
            pl.BlockSpec((1, tm, D_MODEL), lambda i, t: (i, t, 0)),
            pl.BlockSpec((1, D_MODEL), lambda i, t: (0, 0)),
            *[weight_spec(w) for w in weights],
            pl.BlockSpec((tm, LANES), lambda i, t: (t, 0)),
            pl.BlockSpec((tm, LANES), lambda i, t: (t, 0)),
        ],
        out_specs=[spec for _, spec in outs],
        out_shape=[shape for shape, _ in outs],
        scratch_shapes=[pltpu.VMEM((n_slots, SUB_TILE, LANES), jnp.int32),
                        pltpu.VMEM((n_slots, SUB_TILE, LANES), jnp.int32)],
        compiler_params=pltpu.CompilerParams(
            dimension_semantics=("arbitrary", "arbitrary"), vmem_limit_bytes=VMEM_LIMIT),
        name="inproj",
    )(x, g, *weights, cos, sin)


def _merge(a, b):
    acc_a, m_a, l_a = a
    acc_b, m_b, l_b = b
    m = jnp.maximum(m_a, m_b)
    w_a = jnp.exp2(m_a - m)
    w_b = jnp.exp2(m_b - m)
    return w_a * acc_a + w_b * acc_b, m, w_a * l_a + w_b * l_b


def _attn_kernel(*refs, n_chunks):
    halo = n_chunks > 1
    it = iter(refs)
    q_refs = [next(it) for _ in VIEWS]
    k_refs = [[next(it) for _ in range(3 if halo else 1)] for _ in VIEWS]
    v_refs = [[next(it) for _ in range(3 if halo else 1)] for _ in VIEWS]
    ga_ref = next(it)
    bias_ref = next(it)
    if not halo:
        filler_ref = next(it)
        k_refs = [[parts[0], filler_ref, filler_ref] for parts in k_refs]
        v_refs = [[parts[0], filler_ref, filler_ref] for parts in v_refs]
    o_ref = next(it)
    scratch = tuple(it)

    ci = pl.program_id(1)
    first_chunk = (ci == 0).astype(jnp.int32)
    last_chunk = (ci == n_chunks - 1).astype(jnp.int32)

    word_lane = lax.broadcasted_iota(jnp.int32, (Q_BLOCK // 2, LANES), 1)
    all_bits = jnp.full((Q_BLOCK // 2, LANES), 0xFFFFFFFF, jnp.uint32)
    no_bits = jnp.zeros((Q_BLOCK // 2, LANES), jnp.uint32)
    in_head0 = (word_lane % (LANES // 2)) < HEAD_DIM // 2
    keep_head0 = jnp.where(in_head0, all_bits, no_bits)
    keep_head1 = jnp.where(in_head0, no_bits, all_bits)
    out_head0 = lax.broadcasted_iota(jnp.int32, (Q_BLOCK, LANES), 1) < HEAD_DIM
    ones = jnp.ones((K_BLOCK, LANES), BF16)

    def window(parts, hp, residue, lo, hi):
        lanes = slice(residue * LANES, (residue + 1) * LANES)
        rows = parts[0].shape[2]
        halo_hp = hp if halo else 0
        pieces = []
        if lo < 0:
            pieces.append(parts[1][0, halo_hp, 0, HALF + lo:HALF, lanes])
        pieces.append(parts[0][0, hp, max(lo, 0):min(hi, rows), lanes])
        if hi > rows:
            pieces.append(parts[2][0, halo_hp, 0, 0:hi - rows, lanes])
        return pieces

    def rows_of(pieces):
        return pieces[0] if len(pieces) == 1 else jnp.concatenate(pieces, axis=0)

    def block(q, kwin, vwin, bias_index):
        q = pltpu.bitcast(q, jnp.uint32)
        bias = bias_ref[bias_index]
        q2 = jnp.concatenate([pltpu.bitcast(q & keep_head0, BF16), pltpu.bitcast(q & keep_head1, BF16)], axis=0)
        s = lax.dot_general(q2, kwin, (((1,), (1,)), ((), ())), preferred_element_type=F32)
        ms, ps = [], []
        for h in range(2):
            sh = s[h * Q_BLOCK:(h + 1) * Q_BLOCK] + bias
            m = jnp.max(sh, axis=-1, keepdims=True)
            ps.append(jnp.exp2(sh - m).astype(BF16))
            ms.append(jnp.broadcast_to(m, (Q_BLOCK, LANES)))
        pv = jnp.dot(jnp.concatenate(ps, axis=0), jnp.concatenate([vwin, ones], axis=1),
                     preferred_element_type=F32)
        head0, head1 = pv[:Q_BLOCK], pv[Q_BLOCK:]
        return (jnp.where(out_head0, head0[:, :LANES], head1[:, :LANES]),
                jnp.where(out_head0, ms[0], ms[1]),
                jnp.where(out_head0, head0[:, LANES:], head1[:, LANES:]))

    def bias_index(base, is_first_block, is_last_block):
        first = first_chunk if is_first_block else 0
        last = last_chunk if is_last_block else 0
        return base + 2 * first + last

    q4_ref, q16_ref = q_refs
    (k4, k16), (v4, v16) = k_refs, v_refs
    rows16 = CHUNK // 16
    rows4 = CHUNK // 4
    for hp in range(PAIRS_PER_STEP):
        scratch_set = hp % SCRATCH_SETS
        acc_ref, m_ref, l_ref, tok_ref = scratch[4 * scratch_set:4 * scratch_set + 4]

        for r in range(16):
            a, b = divmod(r, 4)
            state = block(q16_ref[0, hp, :, r * LANES:(r + 1) * LANES],
                          rows_of(window(k16, hp, r, -HALF, rows16 + HALF)),
                          rows_of(window(v16, hp, r, -HALF, rows16 + HALF)),
                          bias_index(0, True, True))
            for ref, value in zip((acc_ref, m_ref, l_ref), state):
                ref[b, pl.ds(a, Q_BLOCK, stride=4), :] = value

        for b in range(4):
            for start in range(0, rows4, Q_BLOCK):
                rows = slice(start, start + Q_BLOCK)
                state = block(q4_ref[0, hp, rows, b * LANES:(b + 1) * LANES],
                              rows_of(window(k4, hp, b, start - HALF, start + Q_BLOCK + HALF)),
                              rows_of(window(v4, hp, b, start - HALF, start + Q_BLOCK + HALF)),
                              bias_index(0, start == 0, start == rows4 - Q_BLOCK))
                merged = _merge(state, (acc_ref[b, rows, :], m_ref[b, rows, :], l_ref[b, rows, :]))
                for ref, value in zip((acc_ref, m_ref, l_ref), merged):
                    ref[b, rows, :] = value

        for w0 in range(0, rows4, GROUP_Q_ROWS):
            lo, hi = w0 - GROUP_PAD, w0 - GROUP_PAD + GROUP_K_ROWS
            rows = slice(w0, w0 + GROUP_Q_ROWS)
            q = jnp.concatenate([q4_ref[0, hp, rows, b * LANES:(b + 1) * LANES] for b in range(4)], axis=0)
            kwin = jnp.concatenate([piece for b in range(4) for piece in window(k4, hp, b, lo, hi)], axis=0)
            vwin = jnp.concatenate([piece for b in range(4) for piece in window(v4, hp, b, lo, hi)], axis=0)
            state = block(q, kwin, vwin, bias_index(GROUPED_BIAS, w0 == 0, w0 == rows4 - GROUP_Q_ROWS))
            prev = tuple(jnp.concatenate([ref[b, rows, :] for b in range(4)], axis=0)
                         for ref in (acc_ref, m_ref, l_ref))
            acc, _, l = _merge(state, prev)
            out = acc / l
            tokens = slice(4 * w0, 4 * w0 + Q_BLOCK)
            for b in range(4):
                tok_ref[pl.ds(4 * w0 + b, GROUP_Q_ROWS, stride=4), :] = out[b * GROUP_Q_ROWS:(b + 1) * GROUP_Q_ROWS]
            o_ref[0, hp, tokens, :] = (tok_ref[tokens, :] * ga_ref[0, hp, tokens, :].astype(F32)).astype(BF16)


def _attn(q_views, k_views, v_views, ga, bias):
    b, _, rows, _ = q_views[0].shape
    s = rows * VIEWS[0]
    n_chunks = s // CHUNK
    halo = n_chunks > 1
    args, specs = [], []

    pairs = PAIRS_PER_STEP

    def center(view, d):
        args.append(view)
        specs.append(pl.BlockSpec((1, pairs, CHUNK // d, LANES * d), lambda i, c, p: (i, p, c, 0)))

    def halos(arr, d):
        blocks_per_chunk = CHUNK // d // HALF
        n_blocks = s // d // HALF
        view = arr.reshape(b, N_HEAD_PAIRS, n_blocks, HALF, LANES * d)
        shape = (1, pairs, 1, HALF, LANES * d)
        args.append(view)
        specs.append(pl.BlockSpec(
            shape, lambda i, c, p: (i, p, jnp.maximum(c * blocks_per_chunk - 1, 0), 0, 0)))
        args.append(view)
        specs.append(pl.BlockSpec(
            shape, lambda i, c, p: (i, p, jnp.minimum((c + 1) * blocks_per_chunk, n_blocks - 1), 0, 0)))

    for view, d in zip(q_views, VIEWS):
        center(view, d)
    for views in (k_views, v_views):
        for view, d in zip(views, VIEWS):
            center(view, d)
            if halo:
                halos(view, d)
    args.append(ga)
    specs.append(pl.BlockSpec((1, pairs, CHUNK, LANES), lambda i, c, p: (i, p, c, 0)))
    args.append(bias)
    specs.append(pl.BlockSpec(bias.shape, lambda i, c, p: (0, 0, 0)))
    if not halo:
        filler_shape = (1, 1, 1, HALF, LANES * VIEWS[-1])
        args.append(jnp.zeros(filler_shape, BF16))
        specs.append(pl.BlockSpec(filler_shape, lambda i, c, p: (0, 0, 0, 0, 0)))

    state = pltpu.VMEM((4, CHUNK // 4, LANES), F32)
    token_order = pltpu.VMEM((CHUNK, LANES), F32)
    return pl.pallas_call(
        functools.partial(_attn_kernel, n_chunks=n_chunks),
        grid=(b, n_chunks, N_HEAD_PAIRS // pairs),
        in_specs=specs,
        out_specs=pl.BlockSpec((1, pairs, CHUNK, LANES), lambda i, c, p: (i, p, c, 0)),
        out_shape=jax.ShapeDtypeStruct((b, N_HEAD_PAIRS, s, LANES), BF16),
        scratch_shapes=[state, state, state, token_order] * SCRATCH_SETS,
        compiler_params=pltpu.CompilerParams(
            dimension_semantics=("arbitrary", "arbitrary", "arbitrary"), vmem_limit_bytes=VMEM_LIMIT),
        name="attn",
    )(*args)


def _pool_bands():
    i = np.arange(POOL_BLOCK)[:, None]
    j = np.arange(POOL_K)[None, :]
    rel = j - POOL_HALO - i
    return np.stack([(rel >= -(w // 2)) & (rel <= w // 2 - 1) for w in POOL_WINDOWS]).astype(np.float32)


def _outproj_kernel(ya_ref, u_ref, up_ref, un_ref, gp_ref, x_ref, band_ref, wp_ref, ps_ref, wo_ref, fg_ref,
                    o_ref, *, seq_len, final):
    tm = OUT_TILE
    t = pl.program_id(1)
    n_t = pl.num_programs(1)
    halo_zero = jnp.zeros((POOL_HALO, D_POOL), BF16)
    ub = jnp.concatenate(
        [jnp.where(t > 0, up_ref[0, 0], halo_zero), u_ref[0], jnp.where(t < n_t - 1, un_ref[0, 0], halo_zero),
         jnp.zeros((POOL_K - POOL_BLOCK - 2 * POOL_HALO, D_POOL), BF16)], axis=0)

    pos = t * tm + lax.broadcasted_iota(jnp.int32, (tm, POOL_GROUP_DIM), 0)
    parts = [ya_ref[0, p] for p in range(N_HEAD_PAIRS)]
    for g, w in enumerate(POOL_WINDOWS):
        lanes = slice(g * POOL_GROUP_DIM, (g + 1) * POOL_GROUP_DIM)
        total = jnp.concatenate(
            [jnp.dot(band_ref[g], ub[r0:r0 + POOL_K, lanes], preferred_element_type=F32)
             for r0 in range(0, tm, POOL_BLOCK)], axis=0)
        count = (jnp.minimum(pos + w // 2, seq_len) - jnp.maximum(pos - w // 2, 0)).astype(F32)
        pooled = total / count - u_ref[0, :, lanes].astype(F32)
        mixed = jnp.dot(pooled.astype(BF16), wp_ref[g], preferred_element_type=F32)
        parts.append((mixed * ps_ref[:, lanes] * gp_ref[0, :, lanes].astype(F32)).astype(BF16))
    y = jnp.concatenate(parts, axis=1)
    out = x_ref[0] + jnp.dot(y, wo_ref[...], preferred_element_type=F32)
    if final:
        inv = lax.rsqrt(jnp.mean(out * out, axis=-1, keepdims=True) + RMS_EPS)
        out = out * inv * fg_ref[...]
    o_ref[0] = out


def _outproj(ya, u, gp, x, bands, wp, ps, wo, layer, fg, final):
    b, s, _ = x.shape
    tm = OUT_TILE
    halo_blocks = s // POOL_HALO
    per_tile = tm // POOL_HALO
    u_halo = u.reshape(b, halo_blocks, POOL_HALO, D_POOL)
    const2 = lambda i, t: (0, 0)
    const3 = lambda i, t: (0, 0, 0)
    return pl.pallas_call(
        functools.partial(_outproj_kernel, seq_len=s, final=final),
        grid=(b, s // tm),
        in_specs=[
            pl.BlockSpec((1, N_HEAD_PAIRS, tm, LANES), lambda i, t: (i, 0, t, 0)),
            pl.BlockSpec((1, tm, D_POOL), lambda i, t: (i, t, 0)),
            pl.BlockSpec((1, 1, POOL_HALO, D_POOL),
                         lambda i, t: (i, jnp.maximum(t * per_tile - 1, 0), 0, 0)),
            pl.BlockSpec((1, 1, POOL_HALO, D_POOL),
                         lambda i, t: (i, jnp.minimum((t + 1) * per_tile, halo_blocks - 1), 0, 0)),
            pl.BlockSpec((1, tm, D_POOL), lambda i, t: (i, t, 0)),
            pl.BlockSpec((1, tm, D_MODEL), lambda i, t: (i, t, 0)),
            pl.BlockSpec(bands.shape, const3),
            pl.BlockSpec(wp.shape, const3),
            pl.BlockSpec((1, D_POOL), const2),
            pl.BlockSpec((None, D_MIX, D_MODEL), lambda i, t: (layer, 0, 0)),
            pl.BlockSpec((1, D_MODEL), const2),
        ],
        out_specs=pl.BlockSpec((1, tm, D_MODEL), lambda i, t: (i, t, 0)),
        out_shape=jax.ShapeDtypeStruct((b, s, D_MODEL), F32),
        compiler_params=pltpu.CompilerParams(
            dimension_semantics=("arbitrary", "arbitrary"), vmem_limit_bytes=VMEM_LIMIT),
        name="outproj",
    )(ya, u, u_halo, u_halo, gp, x, bands, wp, ps, wo, fg)


def _rope_tables(seq_len):
    inv_freq = ROPE_THETA ** (-jnp.arange(0, HEAD_DIM, 2, dtype=F32) / HEAD_DIM)
    ang = jnp.arange(seq_len, dtype=F32)[:, None] * inv_freq[None, :]
    cos, sin = jnp.cos(ang), jnp.sin(ang)
    return (jnp.concatenate([cos, cos, cos, cos], axis=1),
            jnp.concatenate([-sin, -sin, sin, sin], axis=1))


def _trunk(x, norm_g, w_in, w_pool, pool_scale, w_out, final_norm_g, bias, bands):
    depth = norm_g.shape[0]
    cos, sin = _rope_tables(x.shape[1])
    fg = final_norm_g.reshape(1, D_MODEL)
    for i in range(depth):
        *qkv, ga, u, gp = _inproj(x, norm_g[i].reshape(1, D_MODEL), w_in, i, cos, sin)
        ya = _attn(qkv[0:2], qkv[2:4], qkv[4:6], ga, bias)
        x = _outproj(ya, u, gp, x, bands, w_pool[i], pool_scale[i].reshape(1, D_POOL), w_out, i, fg,
                     final=(i == depth - 1))
    return x


def kernel(x_prompt, x_sample, norm_g, w_in, w_pool, pool_scale, w_out, final_norm_g):
    order = _rope_column_order()[:D_ATTN]
    w_in = (w_in[:, :, :D_ATTN][:, :, order].astype(BF16), w_in[:, :, D_ATTN:2 * D_ATTN][:, :, order].astype(BF16),
            w_in[:, :, 2 * D_ATTN:].astype(BF16))
    w_pool = w_pool.astype(BF16)
    w_out = w_out.astype(BF16)
    bias = jnp.asarray(_band_bias())
    bands = jnp.asarray(_pool_bands(), dtype=BF16)
    run = functools.partial(_trunk, norm_g=norm_g, w_in=w_in, w_pool=w_pool, pool_scale=pool_scale,
                            w_out=w_out, final_norm_g=final_norm_g, bias=bias, bands=bands)
    return (run(x_prompt), run(x_sample))
```
